```python
import jax, jax.numpy as jnp
from jax import lax
import numpy as np

D_MODEL = 2048
BATCH = 16
SEQ = 2048
DEPTH = 4

D_MIX = 2 * D_MODEL
SSD_WIDTH = D_MIX // 2
SSD_HEAD_DIM = 64
SSD_HEADS = SSD_WIDTH // SSD_HEAD_DIM
SSD_GROUPS = 4
SSD_STATE = 128
SSD_CONV = 5
SSD_CHUNK = 128
SSD_BC = SSD_GROUPS * SSD_STATE
MLSTM_WIDTH = D_MIX - SSD_WIDTH
MLSTM_HEADS = 8
MLSTM_DV = MLSTM_WIDTH // MLSTM_HEADS
MLSTM_DK = MLSTM_DV // 2
MLSTM_CHUNK = 64
D_FF = 5632
FFN_CONV = 3
EPS = 1e-6
SPLIT_SIZES = (SSD_WIDTH, SSD_WIDTH, SSD_BC, SSD_BC, 2 * SSD_HEADS,
               MLSTM_HEADS * MLSTM_DK, MLSTM_HEADS * MLSTM_DK, MLSTM_WIDTH, MLSTM_WIDTH,
               2 * MLSTM_HEADS, 2 * MLSTM_HEADS)
D_IN_PROJ = sum(SPLIT_SIZES)
SSD_CONV_DIM = SSD_WIDTH + 2 * SSD_BC

kernel_name = "hymba_style_ssd_mlstm_convffn_encoder"


def rmsnorm(x, w):
    xf = x.astype(jnp.float32)
    xf = xf * lax.rsqrt(jnp.mean(xf * xf, axis=-1, keepdims=True) + EPS)
    return (xf * w.astype(jnp.float32)).astype(x.dtype)


def dwconv(x, w, b):
    k, c = w.shape
    y = lax.conv_general_dilated(x, w[:, None, :].astype(x.dtype), window_strides=(1,),
                                 padding=[(k // 2, k // 2)],
                                 dimension_numbers=('NWC', 'WIO', 'NWC'),
                                 feature_group_count=c)
    return y + b.astype(x.dtype)


def ssd_scan(x, dt, a, bm, cm):
    bsz, seqlen, h, p = x.shape
    g, n = bm.shape[-2:]
    r = h // g
    nc = seqlen // SSD_CHUNK
    x = x.reshape(bsz, nc, SSD_CHUNK, g, r, p)
    dt = dt.reshape(bsz, nc, SSD_CHUNK, g, r)
    bm = bm.reshape(bsz, nc, SSD_CHUNK, g, n)
    cm = cm.reshape(bsz, nc, SSD_CHUNK, g, n)
    a_dt = jnp.moveaxis(dt * a.reshape(g, r), 2, -1)
    a_cs = jnp.cumsum(a_dt, axis=-1)
    xdt = x * dt[..., None]
    mask = np.tril(np.ones((SSD_CHUNK, SSD_CHUNK), dtype=bool))
    seg = a_cs[..., :, None] - a_cs[..., None, :]
    decay = jnp.exp(jnp.where(mask, seg, -jnp.inf))
    scores = jnp.einsum('bclgn,bcsgn->bcgls', cm, bm)
    y_diag = jnp.einsum('bcgrls,bcsgrp->bclgrp', scores[:, :, :, None] * decay, xdt)
    decay_states = jnp.moveaxis(jnp.exp(a_cs[..., -1:] - a_cs), -1, 2)
    states = jnp.einsum('bclgn,bclgrp->bcgrpn', bm, xdt * decay_states[..., None])
    chunk_sum = a_cs[..., -1]
    cs_incl = jnp.cumsum(chunk_sum, axis=1)
    cs_excl = cs_incl - chunk_sum
    seg_c = cs_excl[:, :, None] - cs_incl[:, None, :]
    cmask = np.tril(np.ones((nc, nc), dtype=bool), -1)[:, :, None, None]
    dec_c = jnp.exp(jnp.where(cmask, seg_c, -jnp.inf))
    init_states = jnp.einsum('bzcgr,bcgrpn->bzgrpn', dec_c, states)
    state_decay_out = jnp.moveaxis(jnp.exp(a_cs), -1, 2)
    y_off = jnp.einsum('bclgn,bcgrpn->bclgrp', cm, init_states) * state_decay_out[..., None]
    return (y_diag + y_off).reshape(bsz, seqlen, h, p)


def ssd_mixer(xs, z, bm, cm, dt_pre, conv_w, conv_b, dt_bias, a_log, d_skip, norm_w):
    out_dtype = xs.dtype
    bsz, seqlen = xs.shape[:2]
    xbc = jax.nn.silu(dwconv(jnp.concatenate([xs, bm, cm], axis=-1), conv_w, conv_b))
    xs, bm, cm = jnp.split(xbc.astype(jnp.float32), [SSD_WIDTH, SSD_WIDTH + SSD_BC], axis=-1)
    xh = xs.reshape(bsz, seqlen, SSD_HEADS, SSD_HEAD_DIM)
    bm = bm.reshape(bsz, seqlen, SSD_GROUPS, SSD_STATE)
    cm = cm.reshape(bsz, seqlen, SSD_GROUPS, SSD_STATE)
    dt = jax.nn.softplus(dt_pre.astype(jnp.float32).reshape(bsz, seqlen, 2, SSD_HEADS)
                         + dt_bias.astype(jnp.float32))
    a = -jnp.exp(a_log.astype(jnp.float32))
    y_f = ssd_scan(xh, dt[:, :, 0], a[0], bm, cm)
    y_b = jnp.flip(ssd_scan(jnp.flip(xh, 1), jnp.flip(dt[:, :, 1], 1), a[1],
                            jnp.flip(bm, 1), jnp.flip(cm, 1)), 1)
    y = y_f + y_b + xh * d_skip.astype(jnp.float32)[:, None]
    y = y.reshape(bsz, seqlen, SSD_WIDTH) * jax.nn.silu(z.astype(jnp.float32))
    y = y.reshape(bsz, seqlen, SSD_GROUPS, SSD_WIDTH // SSD_GROUPS)
    y = y * lax.rsqrt(jnp.mean(y * y, axis=-1, keepdims=True) + EPS)
    y = y.reshape(bsz, seqlen, SSD_WIDTH) * norm_w.astype(jnp.float32)
    return y.astype(out_dtype)


def mlstm_scan(q, k, v, log_i, log_f):
    bsz, h, seqlen, dk = q.shape
    dv = v.shape[-1]
    nc = seqlen // MLSTM_CHUNK

    def chunks(t):
        return jnp.moveaxis(t.reshape(bsz, h, nc, MLSTM_CHUNK, *t.shape[3:]), 2, 0)

    mask = np.tril(np.ones((MLSTM_CHUNK, MLSTM_CHUNK), dtype=bool))

    def step(carry, inp):
        c_state, n_state, m_state = carry
        q_, k_, v_, li, lf = inp
        b = jnp.cumsum(lf, axis=-1)
        a_intra = jnp.where(mask, b[..., :, None] - b[..., None, :] + li[..., None, :], -jnp.inf)
        a_inter = b + m_state[..., None]
        m_comb = jnp.maximum(jnp.max(a_intra, axis=-1), a_inter)
        w_intra = jnp.exp(a_intra - m_comb[..., None])
        w_inter = jnp.exp(a_inter - m_comb)
        s = jnp.einsum('bhld,bhsd->bhls', q_, k_) * w_intra
        num = (jnp.einsum('bhls,bhsv->bhlv', s, v_)
               + w_inter[..., None] * jnp.einsum('bhld,bhdv->bhlv', q_, c_state))
        den = jnp.sum(s, axis=-1) + w_inter * jnp.einsum('bhld,bhd->bhl', q_, n_state)
        h_out = num / jnp.maximum(jnp.abs(den), jnp.exp(-m_comb))[..., None]
        b_last = b[..., -1]
        a_state = b_last[..., None] - b + li
        m_new = jnp.maximum(b_last + m_state, jnp.max(a_state, axis=-1))
        w_state = jnp.exp(a_state - m_new[..., None])
        dec = jnp.exp(b_last + m_state - m_new)
        kw = k_ * w_state[..., None]
        c_new = dec[..., None, None] * c_state + jnp.einsum('bhsd,bhsv->bhdv', kw, v_)
        n_new = dec[..., None] * n_state + jnp.sum(kw, axis=2)
        return (c_new, n_new, m_new), h_out

    init = (jnp.zeros((bsz, h, dk, dv), jnp.float32), jnp.zeros((bsz, h, dk), jnp.float32),
            jnp.full((bsz, h), -1e30, jnp.float32))
    _, hs = lax.scan(step, init, (chunks(q), chunks(k), chunks(v), chunks(log_i), chunks(log_f)))
    return jnp.moveaxis(hs, 0, 2).reshape(bsz, h, seqlen, dv)


def mlstm_mixer(q, k, v, o_pre, i_pre, f_pre, i_bias, f_bias, norm_w):
    out_dtype = q.dtype
    bsz, seqlen = q.shape[:2]

    def heads(t, d):
        return jnp.swapaxes(t.astype(jnp.float32).reshape(bsz, seqlen, MLSTM_HEADS, d), 1, 2)

    qh = heads(q, MLSTM_DK) * (MLSTM_DK ** -0.5)
    kh = heads(k, MLSTM_DK)
    vh = heads(v, MLSTM_DV)
    gate_shape = (bsz, seqlen, 2, MLSTM_HEADS)
    log_i = jnp.transpose(i_pre.astype(jnp.float32).reshape(gate_shape)
                          + i_bias.astype(jnp.float32), (2, 0, 3, 1))
    log_f = jnp.transpose(jax.nn.log_sigmoid(f_pre.astype(jnp.float32).reshape(gate_shape)
                                             + f_bias.astype(jnp.float32)), (2, 0, 3, 1))
    h_f = mlstm_scan(qh, kh, vh, log_i[0], log_f[0])
    h_b = jnp.flip(mlstm_scan(jnp.flip(qh, 2), jnp.flip(kh, 2), jnp.flip(vh, 2),
                              jnp.flip(log_i[1], 2), jnp.flip(log_f[1], 2)), 2)
    hsum = h_f + h_b
    hsum = hsum * lax.rsqrt(jnp.mean(hsum * hsum, axis=-1, keepdims=True) + EPS)
    hsum = jnp.swapaxes(hsum, 1, 2).reshape(bsz, seqlen, MLSTM_WIDTH) * norm_w.astype(jnp.float32)
    return (jax.nn.sigmoid(o_pre.astype(jnp.float32)) * hsum).astype(out_dtype)


def conv_ffn(x, w_up, conv_w, conv_b, w_down):
    u = dwconv(x @ w_up, conv_w, conv_b)
    gate, val = jnp.split(u, 2, axis=-1)
    return (jax.nn.silu(gate) * val) @ w_down


def setup_inputs(seed: int = 0) -> dict:
    key = jax.random.key(seed)
    ks = jax.random.split(key, 20)
    f32 = jnp.float32
    nrm = lambda k, s, sc: jax.random.normal(k, s, f32) * sc
    u_dt = jax.random.uniform(ks[5], (DEPTH, 2, SSD_HEADS), f32)
    dt0 = jnp.exp(u_dt * (jnp.log(0.1) - jnp.log(0.001)) + jnp.log(0.001))
    return {
        "x": nrm(ks[0], (BATCH, SEQ, D_MODEL), 1.0),
        "norm1_w": 1.0 + nrm(ks[1], (DEPTH, D_MODEL), 0.01),
        "w_in": nrm(ks[2], (DEPTH, D_MODEL, D_IN_PROJ), D_MODEL ** -0.5),
        "ssd_conv_w": nrm(ks[3], (DEPTH, SSD_CONV, SSD_CONV_DIM), SSD_CONV ** -0.5),
        "ssd_conv_b": nrm(ks[4], (DEPTH, SSD_CONV_DIM), 0.01),
        "ssd_dt_bias": dt0 + jnp.log(-jnp.expm1(-dt0)),
        "ssd_a_log": jnp.log(jax.random.uniform(ks[6], (DEPTH, 2, SSD_HEADS), f32, 1.0, 16.0)),
        "ssd_d": 1.0 + nrm(ks[7], (DEPTH, SSD_HEADS), 0.01),
        "ssd_norm_w": 1.0 + nrm(ks[8], (DEPTH, SSD_WIDTH), 0.01),
        "mlstm_i_bias": nrm(ks[9], (DEPTH, 2, MLSTM_HEADS), 0.1),
        "mlstm_f_bias": jnp.linspace(3.0, 6.0, MLSTM_HEADS, dtype=f32) + nrm(ks[10], (DEPTH, 2, MLSTM_HEADS), 0.01),
        "mlstm_norm_w": 1.0 + nrm(ks[11], (DEPTH, MLSTM_WIDTH), 0.01),
        "w_out": nrm(ks[12], (DEPTH, D_MIX, D_MODEL), D_MIX ** -0.5),
        "norm2_w": 1.0 + nrm(ks[13], (DEPTH, D_MODEL), 0.01),
        "w_up": nrm(ks[14], (DEPTH, D_MODEL, 2 * D_FF), D_MODEL ** -0.5),
        "ffn_conv_w": nrm(ks[15], (DEPTH, FFN_CONV, 2 * D_FF), FFN_CONV ** -0.5),
        "ffn_conv_b": nrm(ks[16], (DEPTH, 2 * D_FF), 0.01),
        "w_down": nrm(ks[17], (DEPTH, D_FF, D_MODEL), D_FF ** -0.5),
        "norm_f_w": 1.0 + nrm(ks[18], (D_MODEL,), 0.01),
    }


def reference(x, norm1_w, w_in, ssd_conv_w, ssd_conv_b, ssd_dt_bias, ssd_a_log, ssd_d,
              ssd_norm_w, mlstm_i_bias, mlstm_f_bias, mlstm_norm_w, w_out, norm2_w,
              w_up, ffn_conv_w, ffn_conv_b, w_down, norm_f_w):
    split_idx = np.cumsum(SPLIT_SIZES)[:-1].tolist()
    for l in range(DEPTH):
        proj = rmsnorm(x, norm1_w[l]) @ w_in[l]
        xs, z, bm, cm, dt_pre, q, k, v, o_pre, i_pre, f_pre = jnp.split(proj, split_idx, axis=-1)
        y_ssd = ssd_mixer(xs, z, bm, cm, dt_pre, ssd_conv_w[l], ssd_conv_b[l], ssd_dt_bias[l],
                          ssd_a_log[l], ssd_d[l], ssd_norm_w[l])
        y_ml = mlstm_mixer(q, k, v, o_pre, i_pre, f_pre, mlstm_i_bias[l], mlstm_f_bias[l],
                           mlstm_norm_w[l])
        x = x + jnp.concatenate([y_ssd, y_ml], axis=-1) @ w_out[l]
        x = x + conv_ffn(rmsnorm(x, norm2_w[l]), w_up[l], ffn_conv_w[l], ffn_conv_b[l], w_down[l])
    return rmsnorm(x, norm_f_w)
```

```python
import functools

import jax
import jax.numpy as jnp
import numpy as np
from jax import lax
from jax.experimental import pallas as pl
from jax.experimental.pallas import tpu as pltpu

EPS = 1e-6
NEG = -1e30
F32 = jnp.float32
BF16 = jnp.bfloat16

D_MODEL = 2048
SSD_WIDTH = 2048
SSD_HEAD_DIM = 64
SSD_HEADS = 32
SSD_GROUPS = 4
SSD_STATE = 128
SSD_HEADS_PER_GROUP = SSD_HEADS // SSD_GROUPS
SSD_GROUP_WIDTH = SSD_WIDTH // SSD_GROUPS
SSD_CONV = 5
ML_HEADS = 8
ML_DK = 128
ML_DV = 256
ML_WIDTH = ML_HEADS * ML_DV
D_FF = 5632
FFN_CONV = 3
CHUNK = 128
LANES = 128
BF16_ROWS = 16
CONV_PAD = 8
N_MAIN = 2 * SSD_WIDTH + 2 * SSD_GROUPS * SSD_STATE + 2 * ML_HEADS * ML_DK + 2 * ML_WIDTH
N_GATE = (SSD_GROUPS + 1) * LANES
VMEM_LIMIT = 56 * 1024 * 1024


def _params(*sem):
    return pltpu.CompilerParams(dimension_semantics=sem, vmem_limit_bytes=VMEM_LIMIT)


def _silu(x):
    return x * (1.0 / (1.0 + jnp.exp(-x)))


def _softplus(x):
    return jnp.maximum(x, 0.0) + jnp.log(1.0 + jnp.exp(-jnp.abs(x)))


def _dot(a, b):
    return jnp.dot(a, b, preferred_element_type=F32)


def _dot_nt(a, b):
    return lax.dot_general(a, b, (((1,), (1,)), ((), ())), preferred_element_type=F32)


def _split3(x):
    hi = x.astype(BF16)
    r = x - hi.astype(F32)
    mid = r.astype(BF16)
    lo = (r - mid.astype(F32)).astype(BF16)
    return hi, mid, lo


def _cumdot(tri3, x):
    return _dot(tri3, jnp.concatenate(_split3(x), axis=0))


def _tri3(fwd):
    row = lax.broadcasted_iota(jnp.int32, (CHUNK, 3 * CHUNK), 0)
    col = lax.broadcasted_iota(jnp.int32, (CHUNK, 3 * CHUNK), 1) % CHUNK
    keep = (col <= row) if fwd else (col >= row)
    return jnp.where(keep, 1.0, 0.0).astype(BF16)


def _lane_col(x, lane_iota, lane):
    return jnp.sum(jnp.where(lane_iota == lane, x, 0.0), axis=1, keepdims=True)


def _rmsnorm_kernel(x_ref, w_ref, o_ref):
    x = x_ref[...]
    ms = jnp.mean(x * x, axis=1, keepdims=True)
    o_ref[...] = (x * lax.rsqrt(ms + EPS) * w_ref[...]).astype(o_ref.dtype)


def _rmsnorm(x, w, tm):
    m, d = x.shape
    return pl.pallas_call(
        _rmsnorm_kernel,
        grid=(m // tm,),
        in_specs=[pl.BlockSpec((tm, d), lambda i: (i, 0)), pl.BlockSpec((1, d), lambda i: (0, 0))],
        out_specs=pl.BlockSpec((tm, d), lambda i: (i, 0)),
        out_shape=jax.ShapeDtypeStruct((m, d), BF16),
        compiler_params=_params("parallel"),
        name="rmsnorm_in",
    )(x, w.reshape(1, d))


def _inproj_kernel(h_ref, w_ref, wg_ref, p_ref, g_ref):
    h = h_ref[...]
    p_ref[...] = _dot(h, w_ref[...]).astype(p_ref.dtype)

    @pl.when(pl.program_id(1) == 0)
    def _():
        g_ref[...] = _dot(h, wg_ref[...])


def _inproj(h, w_main, w_gate, tm, tn):
    m, d = h.shape
    return pl.pallas_call(
        _inproj_kernel,
        grid=(m // tm, N_MAIN // tn),
        in_specs=[pl.BlockSpec((tm, d), lambda i, j: (i, 0)),
                  pl.BlockSpec((d, tn), lambda i, j: (0, j)),
                  pl.BlockSpec((d, N_GATE), lambda i, j: (0, 0))],
        out_specs=[pl.BlockSpec((tm, tn), lambda i, j: (i, j)),
                   pl.BlockSpec((tm, N_GATE), lambda i, j: (i, 0))],
        out_shape=[jax.ShapeDtypeStruct((m, N_MAIN), BF16),
                   jax.ShapeDtypeStruct((m, N_GATE), F32)],
        compiler_params=_params("parallel", "arbitrary"),
        name="in_proj",
    )(h, w_main, w_gate)


def _ssd_chunk(c, fwd, refs, consts):
    (xc_ref, bc_ref, cc_ref, dt_ref, a_ref, y_ref, s_ref) = refs
    (tri3, mask, lane_iota, left) = consts
    r0 = pl.multiple_of(c * CHUNK, CHUNK)
    rows = pl.ds(r0, CHUNK)
    acs = _cumdot(tri3, a_ref[rows, :])
    acs_t = acs.T
    dt_c = dt_ref[rows, :]
    cc = cc_ref[rows, :]
    b_f32 = bc_ref[rows, :]
    scores = _dot_nt(cc, b_f32.astype(BF16))
    b_t = b_f32.T.astype(BF16)
    xch = xc_ref[rows, :]
    s_prev = s_ref[...]
    y_off = _dot(cc, s_prev.astype(BF16))
    base = 0 if fwd else SSD_HEADS_PER_GROUP
    edge = CHUNK - 1 if fwd else 0
    for q in range(SSD_HEADS_PER_GROUP // 2):
        cols = slice(q * LANES, (q + 1) * LANES)
        col_a, col_dt, ms = [], [], []
        for e in range(2):
            lane = base + 2 * q + e
            ca = _lane_col(acs, lane_iota, lane)
            col_a.append(ca)
            col_dt.append(_lane_col(dt_c, lane_iota, lane))
            seg = ca - acs_t[lane:lane + 1, :]
            decay = jnp.exp(jnp.where(mask, seg, NEG))
            ms.append((scores * decay).astype(BF16))
        a_sel = jnp.where(left, col_a[0], col_a[1])
        dt_sel = jnp.where(left, col_dt[0], col_dt[1])
        last = a_sel[edge:edge + 1, :]
        xdt = xch[:, cols] * dt_sel
        xdt_b = xdt.astype(BF16)
        zero = jnp.zeros_like(xdt_b)
        rhs = jnp.concatenate([jnp.where(left, xdt_b, zero), jnp.where(left, zero, xdt_b)], axis=0)
        y_q = _dot(jnp.concatenate(ms, axis=1), rhs) + jnp.exp(a_sel) * y_off[:, cols]
        y_ref[rows, cols] = y_q
        xw = (xdt * jnp.exp(last - a_sel)).astype(BF16)
        s_ref[:, cols] = jnp.exp(last) * s_prev[:, cols] + _dot(b_t, xw)


def _ssd_kernel(xs_ref, z_ref, b_ref, c_ref, gt_ref, cwx_ref, cwb_ref, cwc_ref, cbx_ref, cbb_ref,
                cbc_ref, hp_ref, dsk_ref, nw_ref, o_ref,
                stage_ref, xc_ref, bc_ref, cc_ref, dt_ref, a_ref, yf_ref, yb_ref, sf_ref, sb_ref):
    seq = xs_ref.shape[0]
    nc = seq // CHUNK
    rb = min(seq, 256)
    gw = SSD_GROUP_WIDTH
    n = SSD_STATE
    width = gw + 2 * n

    stage_ref[0:CONV_PAD, :] = jnp.zeros((CONV_PAD, width), F32)
    stage_ref[CONV_PAD + seq:2 * CONV_PAD + seq, :] = jnp.zeros((CONV_PAD, width), F32)
    for r in range(0, seq, rb):
        dst = slice(CONV_PAD + r, CONV_PAD + r + rb)
        stage_ref[dst, 0:gw] = xs_ref[r:r + rb, :].astype(F32)
        stage_ref[dst, gw:gw + n] = b_ref[r:r + rb, :].astype(F32)
        stage_ref[dst, gw + n:width] = c_ref[r:r + rb, :].astype(F32)

    def conv(r, c0, c1, w_ref, bias_ref):
        acc = jnp.zeros((rb, c1 - c0), F32) + bias_ref[...]
        for k in range(SSD_CONV):
            start = CONV_PAD + r + k - SSD_CONV // 2
            acc = acc + w_ref[k:k + 1, :] * stage_ref[start:start + rb, c0:c1]
        return _silu(acc)

    for r in range(0, seq, rb):
        for c0 in range(0, gw, LANES):
            xc_ref[r:r + rb, c0:c0 + LANES] = conv(
                r, c0, c0 + LANES, cwx_ref.at[:, c0:c0 + LANES], cbx_ref.at[:, c0:c0 + LANES])
        bc_ref[r:r + rb, :] = conv(r, gw, gw + n, cwb_ref, cbb_ref)
        cc_ref[r:r + rb, :] = conv(r, gw + n, width, cwc_ref, cbc_ref).astype(BF16)

    dt = _softplus(gt_ref[...] + hp_ref[0:1, :])
    dt_ref[...] = dt
    a_ref[...] = dt * (-jnp.exp(hp_ref[1:2, :]))

    sf_ref[...] = jnp.zeros_like(sf_ref)
    sb_ref[...] = jnp.zeros_like(sb_ref)

    row = lax.broadcasted_iota(jnp.int32, (CHUNK, CHUNK), 0)
    col = lax.broadcasted_iota(jnp.int32, (CHUNK, CHUNK), 1)
    left = col < SSD_HEAD_DIM
    consts_f = (_tri3(True), col <= row, col, left)
    consts_b = (_tri3(False), col >= row, col, left)
    refs_f = (xc_ref, bc_ref, cc_ref, dt_ref, a_ref, yf_ref, sf_ref)
    refs_b = (xc_ref, bc_ref, cc_ref, dt_ref, a_ref, yb_ref, sb_ref)

    def body(j, carry):
        _ssd_chunk(j, True, refs_f, consts_f)
        _ssd_chunk(nc - 1 - j, False, refs_b, consts_b)
        return carry

    lax.fori_loop(0, nc, body, 0)

    for r in range(0, seq, rb):
        rows = slice(r, r + rb)
        y = yf_ref[rows, :] + yb_ref[rows, :] + xc_ref[rows, :] * dsk_ref[...]
        y = y * _silu(z_ref[rows, :].astype(F32))
        ms = jnp.mean(y * y, axis=1, keepdims=True)
        o_ref[rows, :] = (y * lax.rsqrt(ms + EPS) * nw_ref[...]).astype(o_ref.dtype)


def _ssd_mixer(proj, gates, cw, cb, hp, dsk, nw):
    bsz, seq, _ = proj.shape
    gw, n, g = SSD_GROUP_WIDTH, SSD_STATE, SSD_GROUPS
    xs0, z0 = 0, SSD_WIDTH // gw
    b0 = 2 * SSD_WIDTH // n
    c0 = b0 + g
    cwb0 = SSD_WIDTH // n
    in_specs = [
        pl.BlockSpec((None, seq, gw), lambda b, i: (b, 0, xs0 + i)),
        pl.BlockSpec((None, seq, gw), lambda b, i: (b, 0, z0 + i)),
        pl.BlockSpec((None, seq, n), lambda b, i: (b, 0, b0 + i)),
        pl.BlockSpec((None, seq, n), lambda b, i: (b, 0, c0 + i)),
        pl.BlockSpec((None, seq, LANES), lambda b, i: (b, 0, i)),
        pl.BlockSpec((SSD_CONV, gw), lambda b, i: (0, i)),
        pl.BlockSpec((SSD_CONV, n), lambda b, i: (0, cwb0 + i)),
        pl.BlockSpec((SSD_CONV, n), lambda b, i: (0, cwb0 + g + i)),
        pl.BlockSpec((1, gw), lambda b, i: (0, i)),
        pl.BlockSpec((1, n), lambda b, i: (0, cwb0 + i)),
        pl.BlockSpec((1, n), lambda b, i: (0, cwb0 + g + i)),
        pl.BlockSpec((None, 8, LANES), lambda b, i: (i, 0, 0)),
        pl.BlockSpec((1, gw), lambda b, i: (0, i)),
        pl.BlockSpec((1, gw), lambda b, i: (0, i)),
    ]
    scratch = [
        pltpu.VMEM((seq + 2 * CONV_PAD, gw + 2 * n), F32),
        pltpu.VMEM((seq, gw), F32),
        pltpu.VMEM((seq, n), F32),
        pltpu.VMEM((seq, n), BF16),
        pltpu.VMEM((seq, LANES), F32),
        pltpu.VMEM((seq, LANES), F32),
        pltpu.VMEM((seq, gw), F32),
        pltpu.VMEM((seq, gw), F32),
        pltpu.VMEM((n, gw), F32),
        pltpu.VMEM((n, gw), F32),
    ]
    return pl.pallas_call(
        _ssd_kernel,
        grid=(bsz, g),
        in_specs=in_specs,
        out_specs=pl.BlockSpec((None, seq, gw), lambda b, i: (b, 0, i)),
        out_shape=jax.ShapeDtypeStruct((bsz, seq, SSD_WIDTH), BF16),
        scratch_shapes=scratch,
        compiler_params=_params("parallel", "parallel"),
        name="ssd_mixer",
    )(proj, proj, proj, proj, gates, cw, cw, cw, cb, cb, cb, hp, dsk, nw)


def _mlstm_chunk(c, fwd, head, refs, consts):
    (q_ref, k_ref, v_ref, gi_ref, bs_ref, h_ref, cs_ref, ns_ref, m_ref) = refs
    (mask, lane_iota) = consts
    r0 = pl.multiple_of(c * CHUNK, CHUNK)
    rows = pl.ds(r0, CHUNK)
    d = 0 if fwd else 1
    edge = CHUNK - 1 if fwd else 0
    li = _lane_col(gi_ref[rows, :], lane_iota, d * ML_HEADS + head)
    b = _lane_col(bs_ref[d, rows, :], lane_iota, (2 + d) * ML_HEADS + head)
    b_last = b[edge:edge + 1, :]
    m_state = m_ref[...]
    dcol = li - b
    drow = jnp.broadcast_to(dcol, (CHUNK, CHUNK)).T
    a_intra = jnp.where(mask, b + drow, NEG)
    a_inter = b + m_state
    m_comb = jnp.maximum(jnp.max(a_intra, axis=1, keepdims=True), a_inter)
    w_intra = jnp.exp(a_intra - m_comb)
    w_inter = jnp.exp(a_inter - m_comb)
    q = q_ref[rows, :]
    k = k_ref[rows, :]
    v = v_ref[rows, :]
    s = _dot_nt(q, k) * w_intra
    c_state = cs_ref[...]
    n_state = ns_ref[...]
    num = _dot(s.astype(BF16), v) + w_inter * _dot(q, c_state.astype(BF16))
    den = (jnp.sum(s, axis=1, keepdims=True)
           + w_inter * jnp.sum(q.astype(F32) * n_state, axis=1, keepdims=True))
    h_ref[rows, :] = num * (1.0 / jnp.maximum(jnp.abs(den), jnp.exp(-m_comb)))
    a_state = b_last + dcol
    m_new = jnp.maximum(b_last + m_state, jnp.max(a_state, axis=0, keepdims=True))
    w_state = jnp.exp(a_state - m_new)
    dec = jnp.exp(b_last + m_state - m_new)
    kw = k.astype(F32) * w_state
    cs_ref[...] = dec * c_state + _dot(kw.T.astype(BF16), v)
    ns_ref[...] = dec * n_state + jnp.sum(kw, axis=0, keepdims=True)
    m_ref[...] = m_new


def _mlstm_kernel(q_ref, k_ref, v_ref, o_ref, gt_ref, gb_ref, nw_ref, y_ref,
                  gi_ref, bs_ref, hf_ref, hb_ref, cf_ref, cb_ref, nf_ref, nb_ref, mf_ref, mb_ref):
    seq = q_ref.shape[0]
    nc = seq // CHUNK
    rb = min(seq, 256)
    head = pl.program_id(1)

    @pl.when(head == 0)
    def _():
        tri_f = _tri3(True)
        tri_b = _tri3(False)
        for c in range(nc):
            rows = slice(c * CHUNK, (c + 1) * CHUNK)
            g = gt_ref[rows, :] + gb_ref[...]
            gi_ref[rows, :] = g
            lf = -_softplus(-g)
            bs_ref[0, rows, :] = _cumdot(tri_f, lf)
            bs_ref[1, rows, :] = _cumdot(tri_b, lf)

    for ref in (cf_ref, cb_ref, nf_ref, nb_ref):
        ref[...] = jnp.zeros_like(ref)
    mf_ref[...] = jnp.full(mf_ref.shape, NEG, F32)
    mb_ref[...] = jnp.full(mb_ref.shape, NEG, F32)

    row = lax.broadcasted_iota(jnp.int32, (CHUNK, CHUNK), 0)
    col = lax.broadcasted_iota(jnp.int32, (CHUNK, CHUNK), 1)
    refs_f = (q_ref, k_ref, v_ref, gi_ref, bs_ref, hf_ref, cf_ref, nf_ref, mf_ref)
    refs_b = (q_ref, k_ref, v_ref, gi_ref, bs_ref, hb_ref, cb_ref, nb_ref, mb_ref)

    def body(j, carry):
        _mlstm_chunk(j, True, head, refs_f, (col <= row, col))
        _mlstm_chunk(nc - 1 - j, False, head, refs_b, (col >= row, col))
        return carry

    lax.fori_loop(0, nc, body, 0)

    for r in range(0, seq, rb):
        rows = slice(r, r + rb)
        hs = hf_ref[rows, :] + hb_ref[rows, :]
        ms = jnp.mean(hs * hs, axis=1, keepdims=True)
        hn = hs * lax.rsqrt(ms + EPS) * nw_ref[...]
        gate = 1.0 / (1.0 + jnp.exp(-o_ref[rows, :].astype(F32)))
        y_ref[rows, :] = (gate * hn).astype(y_ref.dtype)


def _mlstm_mixer(proj, gates, gbias, nw):
    bsz, seq, _ = proj.shape
    q0 = (2 * SSD_WIDTH + 2 * SSD_GROUPS * SSD_STATE) // ML_DK
    k0 = q0 + ML_HEADS
    v0 = (2 * SSD_WIDTH + 2 * SSD_GROUPS * SSD_STATE + 2 * ML_HEADS * ML_DK) // ML_DV
    o0 = v0 + ML_HEADS
    in_specs = [
        pl.BlockSpec((None, seq, ML_DK), lambda b, h: (b, 0, q0 + h)),
        pl.BlockSpec((None, seq, ML_DK), lambda b, h: (b, 0, k0 + h)),
        pl.BlockSpec((None, seq, ML_DV), lambda b, h: (b, 0, v0 + h)),
        pl.BlockSpec((None, seq, ML_DV), lambda b, h: (b, 0, o0 + h)),
        pl.BlockSpec((None, seq, LANES), lambda b, h: (b, 0, SSD_GROUPS)),
        pl.BlockSpec((1, LANES), lambda b, h: (0, 0)),
        pl.BlockSpec((1, ML_DV), lambda b, h: (0, h)),
    ]
    scratch = [
        pltpu.VMEM((seq, LANES), F32),
        pltpu.VMEM((2, seq, LANES), F32),
        pltpu.VMEM((seq, ML_DV), F32),
        pltpu.VMEM((seq, ML_DV), F32),
        pltpu.VMEM((ML_DK, ML_DV), F32),
        pltpu.VMEM((ML_DK, ML_DV), F32),
        pltpu.VMEM((1, ML_DK), F32),
        pltpu.VMEM((1, ML_DK), F32),
        pltpu.VMEM((1, 1), F32),
        pltpu.VMEM((1, 1), F32),
    ]
    return pl.pallas_call(
        _mlstm_kernel,
        grid=(bsz, ML_HEADS),
        in_specs=in_specs,
        out_specs=pl.BlockSpec((None, seq, ML_DV), lambda b, h: (b, 0, h)),
        out_shape=jax.ShapeDtypeStruct((bsz, seq, ML_WIDTH), BF16),
        scratch_shapes=scratch,
        compiler_params=_params("parallel", "arbitrary"),
        name="mlstm_mixer",
    )(proj, proj, proj, proj, gates, gbias, nw)


def _outproj_kernel(x_ref, ys_ref, ym_ref, ws_ref, wm_ref, nw_ref, x1_ref, h_ref):
    x1 = x_ref[...] + _dot(ys_ref[...], ws_ref[...]) + _dot(ym_ref[...], wm_ref[...])
    x1_ref[...] = x1
    ms = jnp.mean(x1 * x1, axis=1, keepdims=True)
    h_ref[...] = (x1 * lax.rsqrt(ms + EPS) * nw_ref[...]).astype(h_ref.dtype)


def _outproj(x, y_ssd, y_ml, w_ssd, w_ml, nw, tm):
    m, d = x.shape
    const = lambda i: (0, 0)
    return pl.pallas_call(
        _outproj_kernel,
        grid=(m // tm,),
        in_specs=[pl.BlockSpec((tm, d), lambda i: (i, 0)),
                  pl.BlockSpec((tm, SSD_WIDTH), lambda i: (i, 0)),
                  pl.BlockSpec((tm, ML_WIDTH), lambda i: (i, 0)),
                  pl.BlockSpec((SSD_WIDTH, d), const, pipeline_mode=pl.Buffered(1)),
                  pl.BlockSpec((ML_WIDTH, d), const, pipeline_mode=pl.Buffered(1)),
                  pl.BlockSpec((1, d), const)],
        out_specs=[pl.BlockSpec((tm, d), lambda i: (i, 0)),
                   pl.BlockSpec((tm, d), lambda i: (i, 0))],
        out_shape=[jax.ShapeDtypeStruct((m, d), F32), jax.ShapeDtypeStruct((m, d), BF16)],
        compiler_params=_params("parallel"),
        name="out_proj",
    )(x, y_ssd, y_ml, w_ssd, w_ml, nw)


def _ffn_kernel(tiles_per_seq, h_ref, hp_ref, hn_ref, x_ref, wg_ref, wv_ref, cwg_ref, cwv_ref,
                cbg_ref, cbv_ref, wd_ref, nw_ref, x2_ref, o_ref, hext_ref, ug_ref, uv_ref, acc_ref):
    i = pl.program_id(0)
    j = pl.program_id(1)
    t = h_ref.shape[0]
    halo = BF16_ROWS

    @pl.when(j == 0)
    def _():
        pos = i % tiles_per_seq
        prev = hp_ref[...]
        nxt = hn_ref[...]
        hext_ref[0:halo, :] = jnp.where(pos == 0, jnp.zeros_like(prev), prev)
        hext_ref[halo:halo + t, :] = h_ref[...]
        hext_ref[halo + t:2 * halo + t, :] = jnp.where(pos == tiles_per_seq - 1, jnp.zeros_like(nxt), nxt)
        acc_ref[...] = jnp.zeros_like(acc_ref)

    hext = hext_ref[...]
    ug_ref[...] = _dot(hext, wg_ref[...])
    uv_ref[...] = _dot(hext, wv_ref[...])

    def conv(u_ref, w_ref, b_ref):
        acc = b_ref[...] + w_ref[0:1, :] * u_ref[halo - 1:halo - 1 + t, :]
        acc = acc + w_ref[1:2, :] * u_ref[halo:halo + t, :]
        return acc + w_ref[2:3, :] * u_ref[halo + 1:halo + 1 + t, :]

    act = _silu(conv(ug_ref, cwg_ref, cbg_ref)) * conv(uv_ref, cwv_ref, cbv_ref)
    acc_ref[...] += _dot(act.astype(BF16), wd_ref[...])

    @pl.when(j == pl.num_programs(1) - 1)
    def _():
        x2 = x_ref[...] + acc_ref[...]
        x2_ref[...] = x2
        ms = jnp.mean(x2 * x2, axis=1, keepdims=True)
        o_ref[...] = (x2 * lax.rsqrt(ms + EPS) * nw_ref[...]).astype(o_ref.dtype)


def _ffn(x1, h2, w_up, cw, cb, w_down, nw, seq, t, f, out_dtype):
    m, d = x1.shape
    nj = D_FF // f
    tiles_per_seq = seq // t
    hb = t // BF16_ROWS
    last_blk = m // BF16_ROWS - 1
    in_specs = [
        pl.BlockSpec((t, d), lambda i, j: (i, 0)),
        pl.BlockSpec((BF16_ROWS, d), lambda i, j: (jnp.maximum(i * hb - 1, 0), 0)),
        pl.BlockSpec((BF16_ROWS, d), lambda i, j: (jnp.minimum((i + 1) * hb, last_blk), 0)),
        pl.BlockSpec((t, d), lambda i, j: (i, 0)),
        pl.BlockSpec((d, f), lambda i, j: (0, j)),
        pl.BlockSpec((d, f), lambda i, j: (0, nj + j)),
        pl.BlockSpec((FFN_CONV, f), lambda i, j: (0, j)),
        pl.BlockSpec((FFN_CONV, f), lambda i, j: (0, nj + j)),
        pl.BlockSpec((1, f), lambda i, j: (0, j)),
        pl.BlockSpec((1, f), lambda i, j: (0, nj + j)),
        pl.BlockSpec((f, d), lambda i, j: (j, 0)),
        pl.BlockSpec((1, d), lambda i, j: (0, 0)),
    ]
    scratch = [
        pltpu.VMEM((t + 2 * BF16_ROWS, d), BF16),
        pltpu.VMEM((t + 2 * BF16_ROWS, f), F32),
        pltpu.VMEM((t + 2 * BF16_ROWS, f), F32),
        pltpu.VMEM((t, d), F32),
    ]
    return pl.pallas_call(
        functools.partial(_ffn_kernel, tiles_per_seq),
        grid=(m // t, nj),
        in_specs=in_specs,
        out_specs=[pl.BlockSpec((t, d), lambda i, j: (i, 0)),
                   pl.BlockSpec((t, d), lambda i, j: (i, 0))],
        out_shape=[jax.ShapeDtypeStruct((m, d), F32), jax.ShapeDtypeStruct((m, d), out_dtype)],
        scratch_shapes=scratch,
        compiler_params=_params("parallel", "arbitrary"),
        name="conv_ffn",
    )(h2, h2, h2, x1, w_up, w_up, cw, cw, cb, cb, w_down, nw)


def _pad_lanes(a, width):
    return jnp.pad(a, [(0, 0)] * (a.ndim - 1) + [(0, width - a.shape[-1])])


def _prep_weights(w_in, ssd_dt_bias, ssd_a_log, ssd_d, mlstm_i_bias, mlstm_f_bias):
    depth = w_in.shape[0]
    sizes = (SSD_WIDTH, SSD_WIDTH, SSD_GROUPS * SSD_STATE, SSD_GROUPS * SSD_STATE, 2 * SSD_HEADS,
             ML_HEADS * ML_DK, ML_HEADS * ML_DK, ML_WIDTH, ML_WIDTH, 2 * ML_HEADS, 2 * ML_HEADS)
    offs = np.cumsum((0,) + sizes)
    part = lambda n: w_in[:, :, offs[n]:offs[n + 1]]
    xs, z, bm, cm, dt, q, k, v, o, ig, fg = (part(n) for n in range(len(sizes)))
    w_main = jnp.concatenate([xs, z, bm, cm, q * (ML_DK ** -0.5), k, v, o], axis=-1).astype(BF16)
    hg = SSD_HEADS_PER_GROUP
    dt = dt.reshape(depth, D_MODEL, 2, SSD_GROUPS, hg)
    gate_blocks = [_pad_lanes(jnp.concatenate([dt[:, :, 0, g], dt[:, :, 1, g]], axis=-1), LANES)
                   for g in range(SSD_GROUPS)]
    gate_blocks.append(_pad_lanes(jnp.concatenate([ig, fg], axis=-1), LANES))
    w_gate = jnp.concatenate(gate_blocks, axis=-1).astype(BF16)
    per_group = lambda p: _pad_lanes(
        jnp.swapaxes(p.reshape(depth, 2, SSD_GROUPS, hg), 1, 2).reshape(depth, SSD_GROUPS, 2 * hg), LANES)
    hp = jnp.stack([per_group(ssd_dt_bias), per_group(ssd_a_log)], axis=2)
    hp = jnp.pad(hp, ((0, 0), (0, 0), (0, 6), (0, 0)))
    dsk = jnp.repeat(ssd_d, SSD_HEAD_DIM, axis=-1).reshape(depth, 1, SSD_WIDTH)
    gbias = _pad_lanes(jnp.concatenate([mlstm_i_bias.reshape(depth, 2 * ML_HEADS),
                                        mlstm_f_bias.reshape(depth, 2 * ML_HEADS)], axis=-1),
                       LANES).reshape(depth, 1, LANES)
    return w_main, w_gate, hp, dsk, gbias


def kernel(x, norm1_w, w_in, ssd_conv_w, ssd_conv_b, ssd_dt_bias, ssd_a_log, ssd_d, ssd_norm_w,
           mlstm_i_bias, mlstm_f_bias, mlstm_norm_w, w_out, norm2_w, w_up, ffn_conv_w, ffn_conv_b,
           w_down, norm_f_w):
    bsz, seq, d = x.shape
    depth = w_in.shape[0]
    m = bsz * seq
    assert d == D_MODEL and seq % CHUNK == 0
    tm = min(m, 1024)
    t_ffn = min(seq, 512)

    w_main, w_gate, hp, dsk, gbias = _prep_weights(w_in, ssd_dt_bias, ssd_a_log, ssd_d,
                                                   mlstm_i_bias, mlstm_f_bias)
    w_out_b = w_out.astype(BF16)
    w_up_b = w_up.astype(BF16)
    w_down_b = w_down.astype(BF16)

    xf = x.reshape(m, d)
    h = _rmsnorm(xf, norm1_w[0], tm)
    for l in range(depth):
        proj, gates = _inproj(h, w_main[l], w_gate[l], tm, 1024)
        proj3 = proj.reshape(bsz, seq, N_MAIN)
        gates3 = gates.reshape(bsz, seq, N_GATE)
        y_ssd = _ssd_mixer(proj3, gates3, ssd_conv_w[l], ssd_conv_b[l].reshape(1, -1), hp[l], dsk[l],
                           ssd_norm_w[l].reshape(1, -1))
        y_ml = _mlstm_mixer(proj3, gates3, gbias[l], mlstm_norm_w[l].reshape(1, -1))
        x1, h2 = _outproj(xf, y_ssd.reshape(m, SSD_WIDTH), y_ml.reshape(m, ML_WIDTH),
                          w_out_b[l, :SSD_WIDTH], w_out_b[l, SSD_WIDTH:], norm2_w[l].reshape(1, d),
                          min(m, 512))
        last = l == depth - 1
        nw = norm_f_w if last else norm1_w[l + 1]
        xf, h = _ffn(x1, h2, w_up_b[l], ffn_conv_w[l], ffn_conv_b[l].reshape(1, -1), w_down_b[l],
                     nw.reshape(1, d), seq, t_ffn, 512, F32 if last else BF16)
    return h.reshape(bsz, seq, d)
```

```python
import functools

import jax
import jax.numpy as jnp
import numpy as np
from jax import lax
from jax.experimental import pallas as pl
from jax.experimental.pallas import tpu as pltpu

EPS = 1e-6
NEG = -1e30
F32 = jnp.float32
BF16 = jnp.bfloat16

D_MODEL = 2048
SSD_WIDTH = 2048
SSD_HEAD_DIM = 64
SSD_HEADS = 32
SSD_GROUPS = 4
SSD_STATE = 128
SSD_HEADS_PER_GROUP = SSD_HEADS // SSD_GROUPS
SSD_GROUP_WIDTH = SSD_WIDTH // SSD_GROUPS
SSD_CONV = 5
ML_HEADS = 8
ML_DK = 128
ML_DV = 256
ML_WIDTH = ML_HEADS * ML_DV
D_FF = 5632
FFN_CONV = 3
CHUNK = 128
LANES = 128
BF16_ROWS = 16
CONV_PAD = 8
CONV_ROWS = 64
N_MAIN = 2 * SSD_WIDTH + 2 * SSD_GROUPS * SSD_STATE + 2 * ML_HEADS * ML_DK + 2 * ML_WIDTH
N_GATE = (SSD_GROUPS + 1) * LANES
VMEM_LIMIT = 56 * 1024 * 1024


def _params(*sem):
    return pltpu.CompilerParams(dimension_semantics=sem, vmem_limit_bytes=VMEM_LIMIT)


def _silu(x):
    return x * (1.0 / (1.0 + jnp.exp(-x)))


def _softplus(x):
    return jnp.maximum(x, 0.0) + jnp.log(1.0 + jnp.exp(-jnp.abs(x)))


def _dot(a, b):
    return jnp.dot(a, b, preferred_element_type=F32)


def _dot_nt(a, b):
    return lax.dot_general(a, b, (((1,), (1,)), ((), ())), preferred_element_type=F32)


def _split3(x):
    hi = x.astype(BF16)
    r = x - hi.astype(F32)
    mid = r.astype(BF16)
    lo = (r - mid.astype(F32)).astype(BF16)
    return hi, mid, lo


def _cumdot(tri3, x):
    return _dot(tri3, jnp.concatenate(_split3(x), axis=0))


def _tri3(fwd):
    row = lax.broadcasted_iota(jnp.int32, (CHUNK, 3 * CHUNK), 0)
    col = lax.broadcasted_iota(jnp.int32, (CHUNK, 3 * CHUNK), 1) % CHUNK
    keep = (col <= row) if fwd else (col >= row)
    return jnp.where(keep, 1.0, 0.0).astype(BF16)


def _lane_col(x, lane_iota, lane):
    return jnp.sum(jnp.where(lane_iota == lane, x, 0.0), axis=1, keepdims=True)


def _rmsnorm_kernel(x_ref, w_ref, o_ref):
    x = x_ref[...]
    ms = jnp.mean(x * x, axis=1, keepdims=True)
    o_ref[...] = (x * lax.rsqrt(ms + EPS) * w_ref[...]).astype(o_ref.dtype)


def _rmsnorm(x, w, tm):
    m, d = x.shape
    return pl.pallas_call(
        _rmsnorm_kernel,
        grid=(m // tm,),
        in_specs=[pl.BlockSpec((tm, d), lambda i: (i, 0)), pl.BlockSpec((1, d), lambda i: (0, 0))],
        out_specs=pl.BlockSpec((tm, d), lambda i: (i, 0)),
        out_shape=jax.ShapeDtypeStruct((m, d), BF16),
        compiler_params=_params("parallel"),
        name="rmsnorm_in",
    )(x, w.reshape(1, d))


def _inproj_kernel(h_ref, w_ref, wg_ref, p_ref, g_ref):
    h = h_ref[...]
    p_ref[...] = _dot(h, w_ref[...]).astype(p_ref.dtype)

    @pl.when(pl.program_id(1) == 0)
    def _():
        g_ref[...] = _dot(h, wg_ref[...])


def _inproj(h, w_main, w_gate, tm, tn):
    m, d = h.shape
    return pl.pallas_call(
        _inproj_kernel,
        grid=(m // tm, N_MAIN // tn),
        in_specs=[pl.BlockSpec((tm, d), lambda i, j: (i, 0)),
                  pl.BlockSpec((d, tn), lambda i, j: (0, j)),
                  pl.BlockSpec((d, N_GATE), lambda i, j: (0, 0))],
        out_specs=[pl.BlockSpec((tm, tn), lambda i, j: (i, j)),
                   pl.BlockSpec((tm, N_GATE), lambda i, j: (i, 0))],
        out_shape=[jax.ShapeDtypeStruct((m, N_MAIN), BF16),
                   jax.ShapeDtypeStruct((m, N_GATE), F32)],
        compiler_params=_params("parallel", "arbitrary"),
        name="in_proj",
    )(h, w_main, w_gate)


def _ssd_chunk(c, fwd, refs, consts):
    (xb_ref, bt_ref, cc_ref, acs_ref, rt_ref, y_ref, s_ref) = refs
    (mask, lane_iota, left) = consts
    rows = pl.ds(pl.multiple_of(c * CHUNK, CHUNK), CHUNK)
    acs = acs_ref[rows, :]
    r_t = rt_ref[c]
    cc = cc_ref[rows, :]
    b_t = bt_ref[c]
    scores = _dot(cc, b_t.astype(BF16))
    s_prev = s_ref[...]
    y_off = _dot(cc, s_prev.astype(BF16))
    base = 0 if fwd else SSD_HEADS_PER_GROUP
    edge = CHUNK - 1 if fwd else 0
    for q in range(SSD_HEADS_PER_GROUP // 2):
        cols = slice(q * LANES, (q + 1) * LANES)
        xb = xb_ref[rows, cols]
        zero = jnp.zeros_like(xb)
        xbd = jnp.concatenate([jnp.where(left, xb, zero), jnp.where(left, zero, xb)], axis=0)
        col_a, last, ms, bws = [], [], [], []
        for e in range(2):
            lane = base + 2 * q + e
            ca = _lane_col(acs, lane_iota, lane)
            r_row = r_t[lane:lane + 1, :]
            la = ca[edge:edge + 1, :]
            ms.append((scores * jnp.exp(jnp.where(mask, ca + r_row, NEG))).astype(BF16))
            bws.append((b_t * jnp.exp(r_row + la)).astype(BF16))
            col_a.append(ca)
            last.append(la)
        lhs = jnp.concatenate([jnp.concatenate(ms, axis=1), jnp.concatenate(bws, axis=1)], axis=0)
        res = _dot(lhs, xbd)
        a_sel = jnp.where(left, col_a[0], col_a[1])
        y_ref[rows, cols] = res[:CHUNK] + jnp.exp(a_sel) * y_off[:, cols]
        last_sel = jnp.where(left[0:1, :], last[0], last[1])
        s_ref[:, cols] = jnp.exp(last_sel) * s_prev[:, cols] + res[CHUNK:]


def _ssd_kernel(xs_ref, z_ref, b_ref, c_ref, gt_ref, cwx_ref, cwb_ref, cwc_ref, cbx_ref, cbb_ref,
                cbc_ref, hp_ref, dsk_ref, nw_ref, o_ref,
                stage_ref, xc_ref, xb_ref, bt_ref, cc_ref, acs_ref, rt_ref, yf_ref, yb_ref,
                sf_ref, sb_ref, tri_ref):
    seq = xs_ref.shape[0]
    nc = seq // CHUNK
    rb = min(seq, 256)
    gw = SSD_GROUP_WIDTH
    n = SSD_STATE
    width = gw + 2 * n

    stage_ref[0:CONV_PAD, :] = jnp.zeros((CONV_PAD, width), F32)
    stage_ref[CONV_PAD + seq:2 * CONV_PAD + seq, :] = jnp.zeros((CONV_PAD, width), F32)
    for r in range(0, seq, rb):
        dst = slice(CONV_PAD + r, CONV_PAD + r + rb)
        stage_ref[dst, 0:gw] = xs_ref[r:r + rb, :].astype(F32)
        stage_ref[dst, gw:gw + n] = b_ref[r:r + rb, :].astype(F32)
        stage_ref[dst, gw + n:width] = c_ref[r:r + rb, :].astype(F32)

    cb = CONV_ROWS

    def conv(r, c0, c1, w_ref, bias_ref):
        acc = jnp.zeros((cb, c1 - c0), F32) + bias_ref[...]
        for k in range(SSD_CONV):
            start = CONV_PAD + r + k - SSD_CONV // 2
            acc = acc + w_ref[k:k + 1, :] * stage_ref[start:start + cb, c0:c1]
        return _silu(acc)

    for r in range(0, seq, cb):
        for c0 in range(0, gw, LANES):
            xc = conv(r, c0, c0 + LANES, cwx_ref.at[:, c0:c0 + LANES], cbx_ref.at[:, c0:c0 + LANES])
            xc_ref[r:r + cb, c0:c0 + LANES] = xc
            xb_ref[r:r + cb, c0:c0 + LANES] = xc.astype(BF16)
        cc_ref[r:r + cb, :] = conv(r, gw + n, width, cwc_ref, cbc_ref).astype(BF16)
    for c in range(nc):
        bc = [conv(c * CHUNK + i * cb, gw, gw + n, cwb_ref, cbb_ref) for i in range(CHUNK // cb)]
        bt_ref[c] = jnp.concatenate(bc, axis=0).T

    row = lax.broadcasted_iota(jnp.int32, (CHUNK, CHUNK), 0)
    col = lax.broadcasted_iota(jnp.int32, (CHUNK, CHUNK), 1)
    left = col < SSD_HEAD_DIM

    tri_ref[0] = _tri3(True)
    tri_ref[1] = _tri3(False)

    def cumsum_body(c, carry):
        rows = pl.ds(pl.multiple_of(c * CHUNK, CHUNK), CHUNK)
        dt = _softplus(gt_ref[rows, :] + hp_ref[0:1, :])
        a_dt = dt * (-jnp.exp(hp_ref[1:2, :]))
        acs = jnp.where(col < SSD_HEADS_PER_GROUP, _cumdot(tri_ref[0], a_dt),
                        _cumdot(tri_ref[1], a_dt))
        acs_ref[rows, :] = acs
        rt_ref[c] = (jnp.log(dt) - acs).T
        return carry

    lax.fori_loop(0, nc, cumsum_body, 0, unroll=4)

    sf_ref[...] = jnp.zeros_like(sf_ref)
    sb_ref[...] = jnp.zeros_like(sb_ref)

    consts_f = (col <= row, col, left)
    consts_b = (col >= row, col, left)
    refs_f = (xb_ref, bt_ref, cc_ref, acs_ref, rt_ref, yf_ref, sf_ref)
    refs_b = (xb_ref, bt_ref, cc_ref, acs_ref, rt_ref, yb_ref, sb_ref)

    def body(j, carry):
        _ssd_chunk(j, True, refs_f, consts_f)
        _ssd_chunk(nc - 1 - j, False, refs_b, consts_b)
        return carry

    lax.fori_loop(0, nc, body, 0)

    for r in range(0, seq, rb):
        rows = slice(r, r + rb)
        y = yf_ref[rows, :] + yb_ref[rows, :] + xc_ref[rows, :] * dsk_ref[...]
        y = y * _silu(z_ref[rows, :].astype(F32))
        ms = jnp.mean(y * y, axis=1, keepdims=True)
        o_ref[rows, :] = (y * lax.rsqrt(ms + EPS) * nw_ref[...]).astype(o_ref.dtype)


def _ssd_mixer(proj, gates, cw, cb, hp, dsk, nw):
    bsz, seq, _ = proj.shape
    gw, n, g = SSD_GROUP_WIDTH, SSD_STATE, SSD_GROUPS
    xs0, z0 = 0, SSD_WIDTH // gw
    b0 = 2 * SSD_WIDTH // n
    c0 = b0 + g
    cwb0 = SSD_WIDTH // n
    in_specs = [
        pl.BlockSpec((None, seq, gw), lambda b, i: (b, 0, xs0 + i)),
        pl.BlockSpec((None, seq, gw), lambda b, i: (b, 0, z0 + i)),
        pl.BlockSpec((None, seq, n), lambda b, i: (b, 0, b0 + i)),
        pl.BlockSpec((None, seq, n), lambda b, i: (b, 0, c0 + i)),
        pl.BlockSpec((None, seq, LANES), lambda b, i: (b, 0, i)),
        pl.BlockSpec((SSD_CONV, gw), lambda b, i: (0, i)),
        pl.BlockSpec((SSD_CONV, n), lambda b, i: (0, cwb0 + i)),
        pl.BlockSpec((SSD_CONV, n), lambda b, i: (0, cwb0 + g + i)),
        pl.BlockSpec((1, gw), lambda b, i: (0, i)),
        pl.BlockSpec((1, n), lambda b, i: (0, cwb0 + i)),
        pl.BlockSpec((1, n), lambda b, i: (0, cwb0 + g + i)),
        pl.BlockSpec((None, 8, LANES), lambda b, i: (i, 0, 0)),
        pl.BlockSpec((1, gw), lambda b, i: (0, i)),
        pl.BlockSpec((1, gw), lambda b, i: (0, i)),
    ]
    scratch = [
        pltpu.VMEM((seq + 2 * CONV_PAD, gw + 2 * n), F32),
        pltpu.VMEM((seq, gw), F32),
        pltpu.VMEM((seq, gw), BF16),
        pltpu.VMEM((seq // CHUNK, n, CHUNK), F32),
        pltpu.VMEM((seq, n), BF16),
        pltpu.VMEM((seq, LANES), F32),
        pltpu.VMEM((seq // CHUNK, LANES, CHUNK), F32),
        pltpu.VMEM((seq, gw), F32),
        pltpu.VMEM((seq, gw), F32),
        pltpu.VMEM((n, gw), F32),
        pltpu.VMEM((n, gw), F32),
        pltpu.VMEM((2, CHUNK, 3 * CHUNK), BF16),
    ]
    return pl.pallas_call(
        _ssd_kernel,
        grid=(bsz, g),
        in_specs=in_specs,
        out_specs=pl.BlockSpec((None, seq, gw), lambda b, i: (b, 0, i)),
        out_shape=jax.ShapeDtypeStruct((bsz, seq, SSD_WIDTH), BF16),
        scratch_shapes=scratch,
        compiler_params=_params("parallel", "parallel"),
        name="ssd_mixer",
    )(proj, proj, proj, proj, gates, cw, cw, cw, cb, cb, cb, hp, dsk, nw)


def _mlstm_gates(c, fwd, head, m_state, refs, consts):
    (q_ref, k_t_ref, gi_t_ref, bs_ref, bs_t_ref, sb_ref, wi_ref, en_ref, kw_ref, dec_ref) = refs
    (mask, lane_iota) = consts
    nc = dec_ref.shape[0] // 2
    rows = pl.ds(pl.multiple_of(c * CHUNK, CHUNK), CHUNK)
    d = 0 if fwd else 1
    edge = CHUNK - 1 if fwd else 0
    f_lane = (2 + d) * ML_HEADS + head
    li_row = gi_t_ref[c, pl.ds(d * ML_HEADS + head, 1), :]
    b_row = bs_t_ref[d, c, pl.ds(f_lane, 1), :]
    d_row = li_row - b_row
    b_col = _lane_col(bs_ref[d, rows, :], lane_iota, f_lane)
    dm = jnp.where(mask, d_row, NEG)
    g = jnp.maximum(jnp.max(dm, axis=1, keepdims=True), m_state)
    k_t = k_t_ref[c]
    s = _dot(q_ref[rows, :], k_t) * jnp.exp(dm - g)
    sb_ref[d, rows, :] = s.astype(BF16)
    wi_ref[d, rows, :] = jnp.broadcast_to(jnp.exp(m_state - g), (CHUNK, LANES))
    en_ref[d, rows, :] = jnp.broadcast_to(jnp.exp(-(b_col + g)), (CHUNK, LANES))
    g_last = g[edge:edge + 1, :]
    kw_ref[d, c] = (k_t.astype(F32) * jnp.exp(d_row - g_last)).astype(BF16)
    dec_ref[pl.ds(d * nc + c, 1), :] = jnp.broadcast_to(jnp.exp(m_state - g_last),
                                                        (1, dec_ref.shape[1]))
    return b_col[edge:edge + 1, :] + g_last


def _mlstm_chunk(c, fwd, ones, refs):
    (q_ref, v_ref, sb_ref, wi_ref, en_ref, kw_ref, dec_ref, h_ref, cs_ref) = refs
    nc = dec_ref.shape[0] // 2
    rows = pl.ds(pl.multiple_of(c * CHUNK, CHUNK), CHUNK)
    d = 0 if fwd else 1
    v_aug = jnp.concatenate([v_ref[rows, :], ones], axis=1)
    c_aug = cs_ref[...]
    wi = wi_ref[d, rows, :]
    both = _dot(jnp.concatenate([sb_ref[d, rows, :], kw_ref[d, c]], axis=0), v_aug)
    na = (both[:CHUNK]
          + jnp.concatenate([wi, wi, wi], axis=1) * _dot(q_ref[rows, :], c_aug.astype(BF16)))
    r = 1.0 / jnp.maximum(jnp.abs(na[:, ML_DV:]), en_ref[d, rows, :])
    h_ref[rows, :] = na[:, :ML_DV] * jnp.concatenate([r, r], axis=1)
    cs_ref[...] = dec_ref[pl.ds(d * nc + c, 1), :] * c_aug + both[CHUNK:]


def _mlstm_kernel(q_ref, k_ref, v_ref, o_ref, gt_ref, gb_ref, nw_ref, y_ref,
                  gi_t_ref, bs_ref, bs_t_ref, k_t_ref, sb_ref, wi_ref, en_ref, kw_ref, dec_ref,
                  hf_ref, hb_ref, cf_ref, cb_ref, tri_ref):
    seq = q_ref.shape[0]
    nc = seq // CHUNK
    rb = min(seq, 256)
    head = pl.program_id(1)

    @pl.when(head == 0)
    def _():
        tri_ref[0] = _tri3(True)
        tri_ref[1] = _tri3(False)

        def cumsum_body(c, carry):
            rows = pl.ds(pl.multiple_of(c * CHUNK, CHUNK), CHUNK)
            g = gt_ref[rows, :] + gb_ref[...]
            gi_t_ref[c] = g.T
            lf = -_softplus(-g)
            b_f = _cumdot(tri_ref[0], lf)
            b_b = _cumdot(tri_ref[1], lf)
            bs_ref[0, rows, :] = b_f
            bs_ref[1, rows, :] = b_b
            bs_t_ref[0, c] = b_f.T
            bs_t_ref[1, c] = b_b.T
            return carry

        lax.fori_loop(0, nc, cumsum_body, 0, unroll=2)

    def transpose_body(c, carry):
        rows = pl.ds(pl.multiple_of(c * CHUNK, CHUNK), CHUNK)
        k_t_ref[c] = k_ref[rows, :].astype(F32).T.astype(BF16)
        return carry

    lax.fori_loop(0, nc, transpose_body, 0, unroll=4)
    cf_ref[...] = jnp.zeros_like(cf_ref)
    cb_ref[...] = jnp.zeros_like(cb_ref)

    row = lax.broadcasted_iota(jnp.int32, (CHUNK, CHUNK), 0)
    col = lax.broadcasted_iota(jnp.int32, (CHUNK, CHUNK), 1)
    gate_refs = (q_ref, k_t_ref, gi_t_ref, bs_ref, bs_t_ref, sb_ref, wi_ref, en_ref, kw_ref, dec_ref)

    def gate_body(j, carry):
        m_f = _mlstm_gates(j, True, head, carry[0], gate_refs, (col <= row, col))
        m_b = _mlstm_gates(nc - 1 - j, False, head, carry[1], gate_refs, (col >= row, col))
        return (m_f, m_b)

    m_init = jnp.full((1, 1), NEG, F32)
    lax.fori_loop(0, nc, gate_body, (m_init, m_init))

    ones = jnp.ones((CHUNK, LANES), BF16)
    shared = (q_ref, v_ref, sb_ref, wi_ref, en_ref, kw_ref, dec_ref)

    def body(j, carry):
        _mlstm_chunk(j, True, ones, shared + (hf_ref, cf_ref))
        _mlstm_chunk(nc - 1 - j, False, ones, shared + (hb_ref, cb_ref))
        return carry

    lax.fori_loop(0, nc, body, 0)

    for r in range(0, seq, rb):
        rows = slice(r, r + rb)
        hs = hf_ref[rows, :] + hb_ref[rows, :]
        ms = jnp.mean(hs * hs, axis=1, keepdims=True)
        hn = hs * lax.rsqrt(ms + EPS) * nw_ref[...]
        gate = 1.0 / (1.0 + jnp.exp(-o_ref[rows, :].astype(F32)))
        y_ref[rows, :] = (gate * hn).astype(y_ref.dtype)


def _mlstm_mixer(proj, gates, gbias, nw):
    bsz, seq, _ = proj.shape
    q0 = (2 * SSD_WIDTH + 2 * SSD_GROUPS * SSD_STATE) // ML_DK
    k0 = q0 + ML_HEADS
    v0 = (2 * SSD_WIDTH + 2 * SSD_GROUPS * SSD_STATE + 2 * ML_HEADS * ML_DK) // ML_DV
    o0 = v0 + ML_HEADS
    in_specs = [
        pl.BlockSpec((None, seq, ML_DK), lambda b, h: (b, 0, q0 + h)),
        pl.BlockSpec((None, seq, ML_DK), lambda b, h: (b, 0, k0 + h)),
        pl.BlockSpec((None, seq, ML_DV), lambda b, h: (b, 0, v0 + h)),
        pl.BlockSpec((None, seq, ML_DV), lambda b, h: (b, 0, o0 + h)),
        pl.BlockSpec((None, seq, LANES), lambda b, h: (b, 0, SSD_GROUPS)),
        pl.BlockSpec((1, LANES), lambda b, h: (0, 0)),
        pl.BlockSpec((1, ML_DV), lambda b, h: (0, h)),
    ]
    nc = seq // CHUNK
    aug = ML_DV + LANES
    scratch = [
        pltpu.VMEM((nc, LANES, CHUNK), F32),
        pltpu.VMEM((2, seq, LANES), F32),
        pltpu.VMEM((2, nc, LANES, CHUNK), F32),
        pltpu.VMEM((nc, ML_DK, CHUNK), BF16),
        pltpu.VMEM((2, seq, CHUNK), BF16),
        pltpu.VMEM((2, seq, LANES), F32),
        pltpu.VMEM((2, seq, LANES), F32),
        pltpu.VMEM((2, nc, ML_DK, CHUNK), BF16),
        pltpu.VMEM((2 * nc, aug), F32),
        pltpu.VMEM((seq, ML_DV), F32),
        pltpu.VMEM((seq, ML_DV), F32),
        pltpu.VMEM((ML_DK, aug), F32),
        pltpu.VMEM((ML_DK, aug), F32),
        pltpu.VMEM((2, CHUNK, 3 * CHUNK), BF16),
    ]
    return pl.pallas_call(
        _mlstm_kernel,
        grid=(bsz, ML_HEADS),
        in_specs=in_specs,
        out_specs=pl.BlockSpec((None, seq, ML_DV), lambda b, h: (b, 0, h)),
        out_shape=jax.ShapeDtypeStruct((bsz, seq, ML_WIDTH), BF16),
        scratch_shapes=scratch,
        compiler_params=_params("parallel", "arbitrary"),
        name="mlstm_mixer",
    )(proj, proj, proj, proj, gates, gbias, nw)


def _outproj_kernel(x_ref, ys_ref, ym_ref, ws_ref, wm_ref, nw_ref, x1_ref, h_ref):
    x1 = x_ref[...] + _dot(ys_ref[...], ws_ref[...]) + _dot(ym_ref[...], wm_ref[...])
    x1_ref[...] = x1
    ms = jnp.mean(x1 * x1, axis=1, keepdims=True)
    h_ref[...] = (x1 * lax.rsqrt(ms + EPS) * nw_ref[...]).astype(h_ref.dtype)


def _outproj(x, y_ssd, y_ml, w_ssd, w_ml, nw, tm):
    m, d = x.shape
    const = lambda i: (0, 0)
    return pl.pallas_call(
        _outproj_kernel,
        grid=(m // tm,),
        in_specs=[pl.BlockSpec((tm, d), lambda i: (i, 0)),
                  pl.BlockSpec((tm, SSD_WIDTH), lambda i: (i, 0)),
                  pl.BlockSpec((tm, ML_WIDTH), lambda i: (i, 0)),
                  pl.BlockSpec((SSD_WIDTH, d), const, pipeline_mode=pl.Buffered(1)),
                  pl.BlockSpec((ML_WIDTH, d), const, pipeline_mode=pl.Buffered(1)),
                  pl.BlockSpec((1, d), const)],
        out_specs=[pl.BlockSpec((tm, d), lambda i: (i, 0)),
                   pl.BlockSpec((tm, d), lambda i: (i, 0))],
        out_shape=[jax.ShapeDtypeStruct((m, d), F32), jax.ShapeDtypeStruct((m, d), BF16)],
        compiler_params=_params("parallel"),
        name="out_proj",
    )(x, y_ssd, y_ml, w_ssd, w_ml, nw)


def _ffn_kernel(tiles_per_seq, h_ref, hp_ref, hn_ref, x_ref, wg_ref, wv_ref, cwg_ref, cwv_ref,
                cbg_ref, cbv_ref, wd_ref, nw_ref, x2_ref, o_ref, hext_ref, ug_ref, uv_ref, acc_ref):
    i = pl.program_id(0)
    j = pl.program_id(1)
    t = h_ref.shape[0]
    halo = BF16_ROWS

    @pl.when(j == 0)
    def _():
        pos = i % tiles_per_seq
        prev = hp_ref[...]
        nxt = hn_ref[...]
        hext_ref[0:halo, :] = jnp.where(pos == 0, jnp.zeros_like(prev), prev)
        hext_ref[halo:halo + t, :] = h_ref[...]
        hext_ref[halo + t:2 * halo + t, :] = jnp.where(pos == tiles_per_seq - 1, jnp.zeros_like(nxt), nxt)
        acc_ref[...] = jnp.zeros_like(acc_ref)

    hext = hext_ref[...]
    ug_ref[...] = _dot(hext, wg_ref[...])
    uv_ref[...] = _dot(hext, wv_ref[...])

    def conv(u_ref, w_ref, b_ref):
        acc = b_ref[...] + w_ref[0:1, :] * u_ref[halo - 1:halo - 1 + t, :]
        acc = acc + w_ref[1:2, :] * u_ref[halo:halo + t, :]
        return acc + w_ref[2:3, :] * u_ref[halo + 1:halo + 1 + t, :]

    act = _silu(conv(ug_ref, cwg_ref, cbg_ref)) * conv(uv_ref, cwv_ref, cbv_ref)
    acc_ref[...] += _dot(act.astype(BF16), wd_ref[...])

    @pl.when(j == pl.num_programs(1) - 1)
    def _():
        x2 = x_ref[...] + acc_ref[...]
        x2_ref[...] = x2
        ms = jnp.mean(x2 * x2, axis=1, keepdims=True)
        o_ref[...] = (x2 * lax.rsqrt(ms + EPS) * nw_ref[...]).astype(o_ref.dtype)


def _ffn(x1, h2, w_up, cw, cb, w_down, nw, seq, t, f, out_dtype):
    m, d = x1.shape
    nj = D_FF // f
    tiles_per_seq = seq // t
    hb = t // BF16_ROWS
    last_blk = m // BF16_ROWS - 1
    in_specs = [
        pl.BlockSpec((t, d), lambda i, j: (i, 0)),
        pl.BlockSpec((BF16_ROWS, d), lambda i, j: (jnp.maximum(i * hb - 1, 0), 0)),
        pl.BlockSpec((BF16_ROWS, d), lambda i, j: (jnp.minimum((i + 1) * hb, last_blk), 0)),
        pl.BlockSpec((t, d), lambda i, j: (i, 0)),
        pl.BlockSpec((d, f), lambda i, j: (0, j)),
        pl.BlockSpec((d, f), lambda i, j: (0, nj + j)),
        pl.BlockSpec((FFN_CONV, f), lambda i, j: (0, j)),
        pl.BlockSpec((FFN_CONV, f), lambda i, j: (0, nj + j)),
        pl.BlockSpec((1, f), lambda i, j: (0, j)),
        pl.BlockSpec((1, f), lambda i, j: (0, nj + j)),
        pl.BlockSpec((f, d), lambda i, j: (j, 0)),
        pl.BlockSpec((1, d), lambda i, j: (0, 0)),
    ]
    scratch = [
        pltpu.VMEM((t + 2 * BF16_ROWS, d), BF16),
        pltpu.VMEM((t + 2 * BF16_ROWS, f), F32),
        pltpu.VMEM((t + 2 * BF16_ROWS, f), F32),
        pltpu.VMEM((t, d), F32),
    ]
    return pl.pallas_call(
        functools.partial(_ffn_kernel, tiles_per_seq),
        grid=(m // t, nj),
        in_specs=in_specs,
        out_specs=[pl.BlockSpec((t, d), lambda i, j: (i, 0)),
                   pl.BlockSpec((t, d), lambda i, j: (i, 0))],
        out_shape=[jax.ShapeDtypeStruct((m, d), F32), jax.ShapeDtypeStruct((m, d), out_dtype)],
        scratch_shapes=scratch,
        compiler_params=_params("parallel", "arbitrary"),
        name="conv_ffn",
    )(h2, h2, h2, x1, w_up, w_up, cw, cw, cb, cb, w_down, nw)


def _pad_lanes(a, width):
    return jnp.pad(a, [(0, 0)] * (a.ndim - 1) + [(0, width - a.shape[-1])])


def _prep_weights(w_in, ssd_dt_bias, ssd_a_log, ssd_d, mlstm_i_bias, mlstm_f_bias):
    depth = w_in.shape[0]
    sizes = (SSD_WIDTH, SSD_WIDTH, SSD_GROUPS * SSD_STATE, SSD_GROUPS * SSD_STATE, 2 * SSD_HEADS,
             ML_HEADS * ML_DK, ML_HEADS * ML_DK, ML_WIDTH, ML_WIDTH, 2 * ML_HEADS, 2 * ML_HEADS)
    offs = np.cumsum((0,) + sizes)
    part = lambda n: w_in[:, :, offs[n]:offs[n + 1]]
    xs, z, bm, cm, dt, q, k, v, o, ig, fg = (part(n) for n in range(len(sizes)))
    w_main = jnp.concatenate([xs, z, bm, cm, q * (ML_DK ** -0.5), k, v, o], axis=-1).astype(BF16)
    hg = SSD_HEADS_PER_GROUP
    dt = dt.reshape(depth, D_MODEL, 2, SSD_GROUPS, hg)
    gate_blocks = [_pad_lanes(jnp.concatenate([dt[:, :, 0, g], dt[:, :, 1, g]], axis=-1), LANES)
                   for g in range(SSD_GROUPS)]
    gate_blocks.append(_pad_lanes(jnp.concatenate([ig, fg], axis=-1), LANES))
    w_gate = jnp.concatenate(gate_blocks, axis=-1).astype(BF16)
    per_group = lambda p: _pad_lanes(
        jnp.swapaxes(p.reshape(depth, 2, SSD_GROUPS, hg), 1, 2).reshape(depth, SSD_GROUPS, 2 * hg), LANES)
    hp = jnp.stack([per_group(ssd_dt_bias), per_group(ssd_a_log)], axis=2)
    hp = jnp.pad(hp, ((0, 0), (0, 0), (0, 6), (0, 0)))
    dsk = jnp.repeat(ssd_d, SSD_HEAD_DIM, axis=-1).reshape(depth, 1, SSD_WIDTH)
    gbias = _pad_lanes(jnp.concatenate([mlstm_i_bias.reshape(depth, 2 * ML_HEADS),
                                        mlstm_f_bias.reshape(depth, 2 * ML_HEADS)], axis=-1),
                       LANES).reshape(depth, 1, LANES)
    return w_main, w_gate, hp, dsk, gbias


def kernel(x, norm1_w, w_in, ssd_conv_w, ssd_conv_b, ssd_dt_bias, ssd_a_log, ssd_d, ssd_norm_w,
           mlstm_i_bias, mlstm_f_bias, mlstm_norm_w, w_out, norm2_w, w_up, ffn_conv_w, ffn_conv_b,
           w_down, norm_f_w):
    bsz, seq, d = x.shape
    depth = w_in.shape[0]
    m = bsz * seq
    assert d == D_MODEL and seq % CHUNK == 0
    tm = min(m, 1024)
    t_ffn = min(seq, 512)

    w_main, w_gate, hp, dsk, gbias = _prep_weights(w_in, ssd_dt_bias, ssd_a_log, ssd_d,
                                                   mlstm_i_bias, mlstm_f_bias)
    w_out_b = w_out.astype(BF16)
    w_up_b = w_up.astype(BF16)
    w_down_b = w_down.astype(BF16)

    xf = x.reshape(m, d)
    h = _rmsnorm(xf, norm1_w[0], tm)
    for l in range(depth):
        proj, gates = _inproj(h, w_main[l], w_gate[l], tm, 1024)
        proj3 = proj.reshape(bsz, seq, N_MAIN)
        gates3 = gates.reshape(bsz, seq, N_GATE)
        y_ssd = _ssd_mixer(proj3, gates3, ssd_conv_w[l], ssd_conv_b[l].reshape(1, -1), hp[l], dsk[l],
                           ssd_norm_w[l].reshape(1, -1))
        y_ml = _mlstm_mixer(proj3, gates3, gbias[l], mlstm_norm_w[l].reshape(1, -1))
        x1, h2 = _outproj(xf, y_ssd.reshape(m, SSD_WIDTH), y_ml.reshape(m, ML_WIDTH),
                          w_out_b[l, :SSD_WIDTH], w_out_b[l, SSD_WIDTH:], norm2_w[l].reshape(1, d),
                          min(m, 512))
        last = l == depth - 1
        nw = norm_f_w if last else norm1_w[l + 1]
        xf, h = _ffn(x1, h2, w_up_b[l], ffn_conv_w[l], ffn_conv_b[l].reshape(1, -1), w_down_b[l],
                     nw.reshape(1, d), seq, t_ffn, 512, F32 if last else BF16)
    return h.reshape(bsz, seq, d)
```

```python
import functools

import jax
import jax.numpy as jnp
import numpy as np
from jax import lax
from jax.experimental import pallas as pl
from jax.experimental.pallas import tpu as pltpu

EPS = 1e-6
NEG = -1e30
F32 = jnp.float32
BF16 = jnp.bfloat16

D_MODEL = 2048
SSD_WIDTH = 2048
SSD_HEAD_DIM = 64
SSD_HEADS = 32
SSD_GROUPS = 4
SSD_STATE = 128
SSD_HEADS_PER_GROUP = SSD_HEADS // SSD_GROUPS
SSD_GROUP_WIDTH = SSD_WIDTH // SSD_GROUPS
SSD_CONV = 5
ML_HEADS = 8
ML_DK = 128
ML_DV = 256
ML_WIDTH = ML_HEADS * ML_DV
D_FF = 5632
FFN_CONV = 3
CHUNK = 128
LANES = 128
BF16_ROWS = 16
CONV_PAD = 8
CONV_ROWS = 64
N_MAIN = 2 * SSD_WIDTH + 2 * SSD_GROUPS * SSD_STATE + 2 * ML_HEADS * ML_DK + 2 * ML_WIDTH
N_GATE = (SSD_GROUPS + 1) * LANES
VMEM_LIMIT = 56 * 1024 * 1024


def _params(*sem):
    return pltpu.CompilerParams(dimension_semantics=sem, vmem_limit_bytes=VMEM_LIMIT)


def _silu(x):
    return x * (1.0 / (1.0 + jnp.exp(-x)))


def _softplus(x):
    return jnp.maximum(x, 0.0) + jnp.log(1.0 + jnp.exp(-jnp.abs(x)))


def _dot(a, b):
    return jnp.dot(a, b, preferred_element_type=F32)


def _dot_nt(a, b):
    return lax.dot_general(a, b, (((1,), (1,)), ((), ())), preferred_element_type=F32)


def _split3(x):
    hi = x.astype(BF16)
    r = x - hi.astype(F32)
    mid = r.astype(BF16)
    lo = (r - mid.astype(F32)).astype(BF16)
    return hi, mid, lo


def _cumdot(tri3, x):
    return _dot(tri3, jnp.concatenate(_split3(x), axis=0))


def _tri3(fwd):
    row = lax.broadcasted_iota(jnp.int32, (CHUNK, 3 * CHUNK), 0)
    col = lax.broadcasted_iota(jnp.int32, (CHUNK, 3 * CHUNK), 1) % CHUNK
    keep = (col <= row) if fwd else (col >= row)
    return jnp.where(keep, 1.0, 0.0).astype(BF16)


def _lane_col(x, lane_iota, lane):
    return jnp.sum(jnp.where(lane_iota == lane, x, 0.0), axis=1, keepdims=True)


def _rmsnorm_kernel(x_ref, w_ref, o_ref):
    x = x_ref[...]
    ms = jnp.mean(x * x, axis=1, keepdims=True)
    o_ref[...] = (x * lax.rsqrt(ms + EPS) * w_ref[...]).astype(o_ref.dtype)


def _rmsnorm(x, w, tm):
    m, d = x.shape
    return pl.pallas_call(
        _rmsnorm_kernel,
        grid=(m // tm,),
        in_specs=[pl.BlockSpec((tm, d), lambda i: (i, 0)), pl.BlockSpec((1, d), lambda i: (0, 0))],
        out_specs=pl.BlockSpec((tm, d), lambda i: (i, 0)),
        out_shape=jax.ShapeDtypeStruct((m, d), BF16),
        compiler_params=_params("parallel"),
        name="rmsnorm_in",
    )(x, w.reshape(1, d))


def _inproj_kernel(h_ref, w_ref, wg_ref, p_ref, g_ref):
    h = h_ref[...]
    p_ref[...] = _dot(h, w_ref[...]).astype(p_ref.dtype)

    @pl.when(pl.program_id(1) == 0)
    def _():
        g_ref[...] = _dot(h, wg_ref[...])


def _inproj(h, w_main, w_gate, layer, tm, tn):
    m, d = h.shape
    return pl.pallas_call(
        _inproj_kernel,
        grid=(m // tm, N_MAIN // tn),
        in_specs=[pl.BlockSpec((tm, d), lambda i, j: (i, 0)),
                  pl.BlockSpec((None, d, tn), lambda i, j: (layer, 0, j)),
                  pl.BlockSpec((None, d, N_GATE), lambda i, j: (layer, 0, 0))],
        out_specs=[pl.BlockSpec((tm, tn), lambda i, j: (i, j)),
                   pl.BlockSpec((tm, N_GATE), lambda i, j: (i, 0))],
        out_shape=[jax.ShapeDtypeStruct((m, N_MAIN), BF16),
                   jax.ShapeDtypeStruct((m, N_GATE), F32)],
        compiler_params=_params("parallel", "arbitrary"),
        name="in_proj",
    )(h, w_main, w_gate)


def _ssd_chunk(c, fwd, refs, consts):
    (xb_ref, bt_ref, cc_ref, acs_ref, rt_ref, y_ref, s_ref) = refs
    (mask, lane_iota, left) = consts
    rows = pl.ds(pl.multiple_of(c * CHUNK, CHUNK), CHUNK)
    acs = acs_ref[rows, :]
    r_t = rt_ref[c]
    cc = cc_ref[rows, :]
    b_t = bt_ref[c]
    scores = _dot(cc, b_t.astype(BF16))
    s_prev = s_ref[...]
    y_off = _dot(cc, s_prev.astype(BF16))
    base = 0 if fwd else SSD_HEADS_PER_GROUP
    edge = CHUNK - 1 if fwd else 0
    for q in range(SSD_HEADS_PER_GROUP // 2):
        cols = slice(q * LANES, (q + 1) * LANES)
        xb = xb_ref[rows, cols]
        zero = jnp.zeros_like(xb)
        xbd = jnp.concatenate([jnp.where(left, xb, zero), jnp.where(left, zero, xb)], axis=0)
        col_a, last, ms, bws = [], [], [], []
        for e in range(2):
            lane = base + 2 * q + e
            ca = _lane_col(acs, lane_iota, lane)
            r_row = r_t[lane:lane + 1, :]
            la = ca[edge:edge + 1, :]
            ms.append((scores * jnp.exp(jnp.where(mask, ca + r_row, NEG))).astype(BF16))
            bws.append((b_t * jnp.exp(r_row + la)).astype(BF16))
            col_a.append(ca)
            last.append(la)
        lhs = jnp.concatenate([jnp.concatenate(ms, axis=1), jnp.concatenate(bws, axis=1)], axis=0)
        res = _dot(lhs, xbd)
        a_sel = jnp.where(left, col_a[0], col_a[1])
        y_ref[rows, cols] = res[:CHUNK] + jnp.exp(a_sel) * y_off[:, cols]
        last_sel = jnp.where(left[0:1, :], last[0], last[1])
        s_ref[:, cols] = jnp.exp(last_sel) * s_prev[:, cols] + res[CHUNK:]


def _ssd_kernel(xs_ref, z_ref, b_ref, c_ref, gt_ref, cwx_ref, cwb_ref, cwc_ref, cbx_ref, cbb_ref,
                cbc_ref, hp_ref, dsk_ref, nw_ref, o_ref,
                stage_ref, xc_ref, xb_ref, bt_ref, cc_ref, acs_ref, rt_ref, yf_ref, yb_ref,
                sf_ref, sb_ref, tri_ref):
    seq = xs_ref.shape[0]
    nc = seq // CHUNK
    rb = min(seq, 256)
    gw = SSD_GROUP_WIDTH
    n = SSD_STATE
    width = gw + 2 * n

    stage_ref[0:CONV_PAD, :] = jnp.zeros((CONV_PAD, width), F32)
    stage_ref[CONV_PAD + seq:2 * CONV_PAD + seq, :] = jnp.zeros((CONV_PAD, width), F32)
    for r in range(0, seq, rb):
        dst = slice(CONV_PAD + r, CONV_PAD + r + rb)
        stage_ref[dst, 0:gw] = xs_ref[r:r + rb, :].astype(F32)
        stage_ref[dst, gw:gw + n] = b_ref[r:r + rb, :].astype(F32)
        stage_ref[dst, gw + n:width] = c_ref[r:r + rb, :].astype(F32)

    cb = CONV_ROWS

    def conv(r, c0, c1, w_ref, bias_ref):
        acc = jnp.zeros((cb, c1 - c0), F32) + bias_ref[...]
        for k in range(SSD_CONV):
            start = CONV_PAD + r + k - SSD_CONV // 2
            acc = acc + w_ref[k:k + 1, :] * stage_ref[start:start + cb, c0:c1]
        return _silu(acc)

    for r in range(0, seq, cb):
        for c0 in range(0, gw, LANES):
            xc = conv(r, c0, c0 + LANES, cwx_ref.at[:, c0:c0 + LANES], cbx_ref.at[:, c0:c0 + LANES])
            xc_ref[r:r + cb, c0:c0 + LANES] = xc
            xb_ref[r:r + cb, c0:c0 + LANES] = xc.astype(BF16)
        cc_ref[r:r + cb, :] = conv(r, gw + n, width, cwc_ref, cbc_ref).astype(BF16)
    for c in range(nc):
        bc = [conv(c * CHUNK + i * cb, gw, gw + n, cwb_ref, cbb_ref) for i in range(CHUNK // cb)]
        bt_ref[c] = jnp.concatenate(bc, axis=0).T

    row = lax.broadcasted_iota(jnp.int32, (CHUNK, CHUNK), 0)
    col = lax.broadcasted_iota(jnp.int32, (CHUNK, CHUNK), 1)
    left = col < SSD_HEAD_DIM

    tri_ref[0] = _tri3(True)
    tri_ref[1] = _tri3(False)

    def cumsum_body(c, carry):
        rows = pl.ds(pl.multiple_of(c * CHUNK, CHUNK), CHUNK)
        dt = _softplus(gt_ref[rows, :] + hp_ref[0:1, :])
        a_dt = dt * (-jnp.exp(hp_ref[1:2, :]))
        acs = jnp.where(col < SSD_HEADS_PER_GROUP, _cumdot(tri_ref[0], a_dt),
                        _cumdot(tri_ref[1], a_dt))
        acs_ref[rows, :] = acs
        rt_ref[c] = (jnp.log(dt) - acs).T
        return carry

    lax.fori_loop(0, nc, cumsum_body, 0, unroll=4)

    sf_ref[...] = jnp.zeros_like(sf_ref)
    sb_ref[...] = jnp.zeros_like(sb_ref)

    consts_f = (col <= row, col, left)
    consts_b = (col >= row, col, left)
    refs_f = (xb_ref, bt_ref, cc_ref, acs_ref, rt_ref, yf_ref, sf_ref)
    refs_b = (xb_ref, bt_ref, cc_ref, acs_ref, rt_ref, yb_ref, sb_ref)

    def body(j, carry):
        _ssd_chunk(j, True, refs_f, consts_f)
        _ssd_chunk(nc - 1 - j, False, refs_b, consts_b)
        return carry

    lax.fori_loop(0, nc, body, 0, unroll=2)

    for r in range(0, seq, rb):
        rows = slice(r, r + rb)
        y = yf_ref[rows, :] + yb_ref[rows, :] + xc_ref[rows, :] * dsk_ref[...]
        y = y * _silu(z_ref[rows, :].astype(F32))
        ms = jnp.mean(y * y, axis=1, keepdims=True)
        o_ref[rows, :] = (y * lax.rsqrt(ms + EPS) * nw_ref[...]).astype(o_ref.dtype)


def _ssd_mixer(proj, gates, cw, cb, hp, dsk, nw, layer):
    bsz, seq, _ = proj.shape
    gw, n, g = SSD_GROUP_WIDTH, SSD_STATE, SSD_GROUPS
    xs0, z0 = 0, SSD_WIDTH // gw
    b0 = 2 * SSD_WIDTH // n
    c0 = b0 + g
    cwb0 = SSD_WIDTH // n
    in_specs = [
        pl.BlockSpec((None, seq, gw), lambda b, i: (b, 0, xs0 + i)),
        pl.BlockSpec((None, seq, gw), lambda b, i: (b, 0, z0 + i)),
        pl.BlockSpec((None, seq, n), lambda b, i: (b, 0, b0 + i)),
        pl.BlockSpec((None, seq, n), lambda b, i: (b, 0, c0 + i)),
        pl.BlockSpec((None, seq, LANES), lambda b, i: (b, 0, i)),
        pl.BlockSpec((None, SSD_CONV, gw), lambda b, i: (layer, 0, i)),
        pl.BlockSpec((None, SSD_CONV, n), lambda b, i: (layer, 0, cwb0 + i)),
        pl.BlockSpec((None, SSD_CONV, n), lambda b, i: (layer, 0, cwb0 + g + i)),
        pl.BlockSpec((None, 1, gw), lambda b, i: (layer, 0, i)),
        pl.BlockSpec((None, 1, n), lambda b, i: (layer, 0, cwb0 + i)),
        pl.BlockSpec((None, 1, n), lambda b, i: (layer, 0, cwb0 + g + i)),
        pl.BlockSpec((None, None, 8, LANES), lambda b, i: (layer, i, 0, 0)),
        pl.BlockSpec((None, 1, gw), lambda b, i: (layer, 0, i)),
        pl.BlockSpec((None, 1, gw), lambda b, i: (layer, 0, i)),
    ]
    scratch = [
        pltpu.VMEM((seq + 2 * CONV_PAD, gw + 2 * n), F32),
        pltpu.VMEM((seq, gw), F32),
        pltpu.VMEM((seq, gw), BF16),
        pltpu.VMEM((seq // CHUNK, n, CHUNK), F32),
        pltpu.VMEM((seq, n), BF16),
        pltpu.VMEM((seq, LANES), F32),
        pltpu.VMEM((seq // CHUNK, LANES, CHUNK), F32),
        pltpu.VMEM((seq, gw), F32),
        pltpu.VMEM((seq, gw), F32),
        pltpu.VMEM((n, gw), F32),
        pltpu.VMEM((n, gw), F32),
        pltpu.VMEM((2, CHUNK, 3 * CHUNK), BF16),
    ]
    return pl.pallas_call(
        _ssd_kernel,
        grid=(bsz, g),
        in_specs=in_specs,
        out_specs=pl.BlockSpec((None, seq, gw), lambda b, i: (b, 0, i)),
        out_shape=jax.ShapeDtypeStruct((bsz, seq, SSD_WIDTH), BF16),
        scratch_shapes=scratch,
        compiler_params=_params("parallel", "parallel"),
        name="ssd_mixer",
    )(proj, proj, proj, proj, gates, cw, cw, cw, cb, cb, cb, hp, dsk, nw)


def _mlstm_gates(c, fwd, head, m_state, refs, consts):
    (q_ref, k_t_ref, gi_t_ref, bs_ref, bs_t_ref, sb_ref, wi_ref, en_ref, kw_ref, dec_ref) = refs
    (mask, lane_iota) = consts
    nc = dec_ref.shape[0] // 2
    rows = pl.ds(pl.multiple_of(c * CHUNK, CHUNK), CHUNK)
    d = 0 if fwd else 1
    edge = CHUNK - 1 if fwd else 0
    f_lane = (2 + d) * ML_HEADS + head
    li_row = gi_t_ref[c, pl.ds(d * ML_HEADS + head, 1), :]
    b_row = bs_t_ref[d, c, pl.ds(f_lane, 1), :]
    d_row = li_row - b_row
    b_col = _lane_col(bs_ref[d, rows, :], lane_iota, f_lane)
    dm = jnp.where(mask, d_row, NEG)
    g = jnp.maximum(jnp.max(dm, axis=1, keepdims=True), m_state)
    k_t = k_t_ref[c]
    s = _dot(q_ref[rows, :], k_t) * jnp.exp(dm - g)
    sb_ref[d, rows, :] = s.astype(BF16)
    wi_ref[d, rows, :] = jnp.broadcast_to(jnp.exp(m_state - g), (CHUNK, LANES))
    en_ref[d, rows, :] = jnp.broadcast_to(jnp.exp(-(b_col + g)), (CHUNK, LANES))
    g_last = g[edge:edge + 1, :]
    kw_ref[d, c] = (k_t.astype(F32) * jnp.exp(d_row - g_last)).astype(BF16)
    dec_ref[pl.ds(d * nc + c, 1), :] = jnp.broadcast_to(jnp.exp(m_state - g_last),
                                                        (1, dec_ref.shape[1]))
    return b_col[edge:edge + 1, :] + g_last


def _mlstm_chunk(c, fwd, ones, refs):
    (q_ref, v_ref, sb_ref, wi_ref, en_ref, kw_ref, dec_ref, h_ref, cs_ref) = refs
    nc = dec_ref.shape[0] // 2
    rows = pl.ds(pl.multiple_of(c * CHUNK, CHUNK), CHUNK)
    d = 0 if fwd else 1
    v_aug = jnp.concatenate([v_ref[rows, :], ones], axis=1)
    c_aug = cs_ref[...]
    wi = wi_ref[d, rows, :]
    both = _dot(jnp.concatenate([sb_ref[d, rows, :], kw_ref[d, c]], axis=0), v_aug)
    na = (both[:CHUNK]
          + jnp.concatenate([wi, wi, wi], axis=1) * _dot(q_ref[rows, :], c_aug.astype(BF16)))
    r = 1.0 / jnp.maximum(jnp.abs(na[:, ML_DV:]), en_ref[d, rows, :])
    h_ref[rows, :] = na[:, :ML_DV] * jnp.concatenate([r, r], axis=1)
    cs_ref[...] = dec_ref[pl.ds(d * nc + c, 1), :] * c_aug + both[CHUNK:]


def _mlstm_kernel(q_ref, k_ref, v_ref, o_ref, gt_ref, gb_ref, nw_ref, y_ref,
                  gi_t_ref, bs_ref, bs_t_ref, k_t_ref, sb_ref, wi_ref, en_ref, kw_ref, dec_ref,
                  hf_ref, hb_ref, cf_ref, cb_ref, tri_ref):
    seq = q_ref.shape[0]
    nc = seq // CHUNK
    rb = min(seq, 256)
    head = pl.program_id(1)

    @pl.when(head == 0)
    def _():
        tri_ref[0] = _tri3(True)
        tri_ref[1] = _tri3(False)

        def cumsum_body(c, carry):
            rows = pl.ds(pl.multiple_of(c * CHUNK, CHUNK), CHUNK)
            g = gt_ref[rows, :] + gb_ref[...]
            gi_t_ref[c] = g.T
            lf = -_softplus(-g)
            b_f = _cumdot(tri_ref[0], lf)
            b_b = _cumdot(tri_ref[1], lf)
            bs_ref[0, rows, :] = b_f
            bs_ref[1, rows, :] = b_b
            bs_t_ref[0, c] = b_f.T
            bs_t_ref[1, c] = b_b.T
            return carry

        lax.fori_loop(0, nc, cumsum_body, 0, unroll=2)

    def transpose_body(c, carry):
        rows = pl.ds(pl.multiple_of(c * CHUNK, CHUNK), CHUNK)
        k_t_ref[c] = k_ref[rows, :].astype(F32).T.astype(BF16)
        return carry

    lax.fori_loop(0, nc, transpose_body, 0, unroll=4)
    cf_ref[...] = jnp.zeros_like(cf_ref)
    cb_ref[...] = jnp.zeros_like(cb_ref)

    row = lax.broadcasted_iota(jnp.int32, (CHUNK, CHUNK), 0)
    col = lax.broadcasted_iota(jnp.int32, (CHUNK, CHUNK), 1)
    gate_refs = (q_ref, k_t_ref, gi_t_ref, bs_ref, bs_t_ref, sb_ref, wi_ref, en_ref, kw_ref, dec_ref)

    def gate_body(j, carry):
        m_f = _mlstm_gates(j, True, head, carry[0], gate_refs, (col <= row, col))
        m_b = _mlstm_gates(nc - 1 - j, False, head, carry[1], gate_refs, (col >= row, col))
        return (m_f, m_b)

    m_init = jnp.full((1, 1), NEG, F32)
    lax.fori_loop(0, nc, gate_body, (m_init, m_init), unroll=4)

    ones = jnp.ones((CHUNK, LANES), BF16)
    shared = (q_ref, v_ref, sb_ref, wi_ref, en_ref, kw_ref, dec_ref)

    def body(j, carry):
        _mlstm_chunk(j, True, ones, shared + (hf_ref, cf_ref))
        _mlstm_chunk(nc - 1 - j, False, ones, shared + (hb_ref, cb_ref))
        return carry

    lax.fori_loop(0, nc, body, 0, unroll=2)

    for r in range(0, seq, rb):
        rows = slice(r, r + rb)
        hs = hf_ref[rows, :] + hb_ref[rows, :]
        ms = jnp.mean(hs * hs, axis=1, keepdims=True)
        hn = hs * lax.rsqrt(ms + EPS) * nw_ref[...]
        gate = 1.0 / (1.0 + jnp.exp(-o_ref[rows, :].astype(F32)))
        y_ref[rows, :] = (gate * hn).astype(y_ref.dtype)


def _mlstm_mixer(proj, gates, gbias, nw, layer):
    bsz, seq, _ = proj.shape
    q0 = (2 * SSD_WIDTH + 2 * SSD_GROUPS * SSD_STATE) // ML_DK
    k0 = q0 + ML_HEADS
    v0 = (2 * SSD_WIDTH + 2 * SSD_GROUPS * SSD_STATE + 2 * ML_HEADS * ML_DK) // ML_DV
    o0 = v0 + ML_HEADS
    in_specs = [
        pl.BlockSpec((None, seq, ML_DK), lambda b, h: (b, 0, q0 + h)),
        pl.BlockSpec((None, seq, ML_DK), lambda b, h: (b, 0, k0 + h)),
        pl.BlockSpec((None, seq, ML_DV), lambda b, h: (b, 0, v0 + h)),
        pl.BlockSpec((None, seq, ML_DV), lambda b, h: (b, 0, o0 + h)),
        pl.BlockSpec((None, seq, LANES), lambda b, h: (b, 0, SSD_GROUPS)),
        pl.BlockSpec((None, 1, LANES), lambda b, h: (layer, 0, 0)),
        pl.BlockSpec((None, 1, ML_DV), lambda b, h: (layer, 0, h)),
    ]
    nc = seq // CHUNK
    aug = ML_DV + LANES
    scratch = [
        pltpu.VMEM((nc, LANES, CHUNK), F32),
        pltpu.VMEM((2, seq, LANES), F32),
        pltpu.VMEM((2, nc, LANES, CHUNK), F32),
        pltpu.VMEM((nc, ML_DK, CHUNK), BF16),
        pltpu.VMEM((2, seq, CHUNK), BF16),
        pltpu.VMEM((2, seq, LANES), F32),
        pltpu.VMEM((2, seq, LANES), F32),
        pltpu.VMEM((2, nc, ML_DK, CHUNK), BF16),
        pltpu.VMEM((2 * nc, aug), F32),
        pltpu.VMEM((seq, ML_DV), F32),
        pltpu.VMEM((seq, ML_DV), F32),
        pltpu.VMEM((ML_DK, aug), F32),
        pltpu.VMEM((ML_DK, aug), F32),
        pltpu.VMEM((2, CHUNK, 3 * CHUNK), BF16),
    ]
    return pl.pallas_call(
        _mlstm_kernel,
        grid=(bsz, ML_HEADS),
        in_specs=in_specs,
        out_specs=pl.BlockSpec((None, seq, ML_DV), lambda b, h: (b, 0, h)),
        out_shape=jax.ShapeDtypeStruct((bsz, seq, ML_WIDTH), BF16),
        scratch_shapes=scratch,
        compiler_params=_params("parallel", "arbitrary"),
        name="mlstm_mixer",
    )(proj, proj, proj, proj, gates, gbias, nw)


def _outproj_kernel(x_ref, ys_ref, ym_ref, ws_ref, wm_ref, nw_ref, x1_ref, h_ref):
    x1 = x_ref[...] + _dot(ys_ref[...], ws_ref[...]) + _dot(ym_ref[...], wm_ref[...])
    x1_ref[...] = x1
    ms = jnp.mean(x1 * x1, axis=1, keepdims=True)
    h_ref[...] = (x1 * lax.rsqrt(ms + EPS) * nw_ref[...]).astype(h_ref.dtype)


def _outproj(x, y_ssd, y_ml, w_out, nw, layer, tm):
    m, d = x.shape
    return pl.pallas_call(
        _outproj_kernel,
        grid=(m // tm,),
        in_specs=[pl.BlockSpec((tm, d), lambda i: (i, 0)),
                  pl.BlockSpec((tm, SSD_WIDTH), lambda i: (i, 0)),
                  pl.BlockSpec((tm, ML_WIDTH), lambda i: (i, 0)),
                  pl.BlockSpec((None, SSD_WIDTH, d), lambda i: (layer, 0, 0),
                               pipeline_mode=pl.Buffered(1)),
                  pl.BlockSpec((None, ML_WIDTH, d), lambda i: (layer, 1, 0),
                               pipeline_mode=pl.Buffered(1)),
                  pl.BlockSpec((None, 1, d), lambda i: (layer, 0, 0))],
        out_specs=[pl.BlockSpec((tm, d), lambda i: (i, 0)),
                   pl.BlockSpec((tm, d), lambda i: (i, 0))],
        out_shape=[jax.ShapeDtypeStruct((m, d), F32), jax.ShapeDtypeStruct((m, d), BF16)],
        compiler_params=_params("parallel"),
        name="out_proj",
    )(x, y_ssd, y_ml, w_out, w_out, nw)


def _ffn_kernel(tiles_per_seq, h_ref, hp_ref, hn_ref, x_ref, wg_ref, wv_ref, cwg_ref, cwv_ref,
                cbg_ref, cbv_ref, wd_ref, nw_ref, x2_ref, o_ref, hext_ref, ug_ref, uv_ref, acc_ref):
    i = pl.program_id(0)
    j = pl.program_id(1)
    t = h_ref.shape[0]
    halo = BF16_ROWS

    @pl.when(j == 0)
    def _():
        pos = i % tiles_per_seq
        prev = hp_ref[...]
        nxt = hn_ref[...]
        hext_ref[0:halo, :] = jnp.where(pos == 0, jnp.zeros_like(prev), prev)
        hext_ref[halo:halo + t, :] = h_ref[...]
        hext_ref[halo + t:2 * halo + t, :] = jnp.where(pos == tiles_per_seq - 1, jnp.zeros_like(nxt), nxt)
        acc_ref[...] = jnp.zeros_like(acc_ref)

    hext = hext_ref[...]
    ug_ref[...] = _dot(hext, wg_ref[...])
    uv_ref[...] = _dot(hext, wv_ref[...])

    def conv(u_ref, w_ref, b_ref):
        acc = b_ref[...] + w_ref[0:1, :] * u_ref[halo - 1:halo - 1 + t, :]
        acc = acc + w_ref[1:2, :] * u_ref[halo:halo + t, :]
        return acc + w_ref[2:3, :] * u_ref[halo + 1:halo + 1 + t, :]

    act = _silu(conv(ug_ref, cwg_ref, cbg_ref)) * conv(uv_ref, cwv_ref, cbv_ref)
    acc_ref[...] += _dot(act.astype(BF16), wd_ref[...])

    @pl.when(j == pl.num_programs(1) - 1)
    def _():
        x2 = x_ref[...] + acc_ref[...]
        x2_ref[...] = x2
        ms = jnp.mean(x2 * x2, axis=1, keepdims=True)
        o_ref[...] = (x2 * lax.rsqrt(ms + EPS) * nw_ref[...]).astype(o_ref.dtype)


def _ffn(x1, h2, w_up, cw, cb, w_down, nw, layer, seq, t, f, out_dtype):
    m, d = x1.shape
    nj = D_FF // f
    tiles_per_seq = seq // t
    hb = t // BF16_ROWS
    last_blk = m // BF16_ROWS - 1
    in_specs = [
        pl.BlockSpec((t, d), lambda i, j: (i, 0)),
        pl.BlockSpec((BF16_ROWS, d), lambda i, j: (jnp.maximum(i * hb - 1, 0), 0)),
        pl.BlockSpec((BF16_ROWS, d), lambda i, j: (jnp.minimum((i + 1) * hb, last_blk), 0)),
        pl.BlockSpec((t, d), lambda i, j: (i, 0)),
        pl.BlockSpec((None, d, f), lambda i, j: (layer, 0, j)),
        pl.BlockSpec((None, d, f), lambda i, j: (layer, 0, nj + j)),
        pl.BlockSpec((None, FFN_CONV, f), lambda i, j: (layer, 0, j)),
        pl.BlockSpec((None, FFN_CONV, f), lambda i, j: (layer, 0, nj + j)),
        pl.BlockSpec((None, 1, f), lambda i, j: (layer, 0, j)),
        pl.BlockSpec((None, 1, f), lambda i, j: (layer, 0, nj + j)),
        pl.BlockSpec((None, f, d), lambda i, j: (layer, j, 0)),
        pl.BlockSpec((1, d), lambda i, j: (0, 0)),
    ]
    scratch = [
        pltpu.VMEM((t + 2 * BF16_ROWS, d), BF16),
        pltpu.VMEM((t + 2 * BF16_ROWS, f), F32),
        pltpu.VMEM((t + 2 * BF16_ROWS, f), F32),
        pltpu.VMEM((t, d), F32),
    ]
    return pl.pallas_call(
        functools.partial(_ffn_kernel, tiles_per_seq),
        grid=(m // t, nj),
        in_specs=in_specs,
        out_specs=[pl.BlockSpec((t, d), lambda i, j: (i, 0)),
                   pl.BlockSpec((t, d), lambda i, j: (i, 0))],
        out_shape=[jax.ShapeDtypeStruct((m, d), F32), jax.ShapeDtypeStruct((m, d), out_dtype)],
        scratch_shapes=scratch,
        compiler_params=_params("parallel", "arbitrary"),
        name="conv_ffn",
    )(h2, h2, h2, x1, w_up, w_up, cw, cw, cb, cb, w_down, nw)


def _pad_lanes(a, width):
    return jnp.pad(a, [(0, 0)] * (a.ndim - 1) + [(0, width - a.shape[-1])])


def _prep_weights(w_in, ssd_dt_bias, ssd_a_log, ssd_d, mlstm_i_bias, mlstm_f_bias):
    depth = w_in.shape[0]
    sizes = (SSD_WIDTH, SSD_WIDTH, SSD_GROUPS * SSD_STATE, SSD_GROUPS * SSD_STATE, 2 * SSD_HEADS,
             ML_HEADS * ML_DK, ML_HEADS * ML_DK, ML_WIDTH, ML_WIDTH, 2 * ML_HEADS, 2 * ML_HEADS)
    offs = np.cumsum((0,) + sizes)
    part = lambda n: w_in[:, :, offs[n]:offs[n + 1]]
    xs, z, bm, cm, dt, q, k, v, o, ig, fg = (part(n) for n in range(len(sizes)))
    w_main = jnp.concatenate([xs, z, bm, cm, q * (ML_DK ** -0.5), k, v, o], axis=-1).astype(BF16)
    hg = SSD_HEADS_PER_GROUP
    dt = dt.reshape(depth, D_MODEL, 2, SSD_GROUPS, hg)
    gate_blocks = [_pad_lanes(jnp.concatenate([dt[:, :, 0, g], dt[:, :, 1, g]], axis=-1), LANES)
                   for g in range(SSD_GROUPS)]
    gate_blocks.append(_pad_lanes(jnp.concatenate([ig, fg], axis=-1), LANES))
    w_gate = jnp.concatenate(gate_blocks, axis=-1).astype(BF16)
    per_group = lambda p: _pad_lanes(
        jnp.swapaxes(p.reshape(depth, 2, SSD_GROUPS, hg), 1, 2).reshape(depth, SSD_GROUPS, 2 * hg), LANES)
    hp = jnp.stack([per_group(ssd_dt_bias), per_group(ssd_a_log)], axis=2)
    hp = jnp.pad(hp, ((0, 0), (0, 0), (0, 6), (0, 0)))
    dsk = jnp.repeat(ssd_d, SSD_HEAD_DIM, axis=-1).reshape(depth, 1, SSD_WIDTH)
    gbias = _pad_lanes(jnp.concatenate([mlstm_i_bias.reshape(depth, 2 * ML_HEADS),
                                        mlstm_f_bias.reshape(depth, 2 * ML_HEADS)], axis=-1),
                       LANES).reshape(depth, 1, LANES)
    return w_main, w_gate, hp, dsk, gbias


def kernel(x, norm1_w, w_in, ssd_conv_w, ssd_conv_b, ssd_dt_bias, ssd_a_log, ssd_d, ssd_norm_w,
           mlstm_i_bias, mlstm_f_bias, mlstm_norm_w, w_out, norm2_w, w_up, ffn_conv_w, ffn_conv_b,
           w_down, norm_f_w):
    bsz, seq, d = x.shape
    depth = w_in.shape[0]
    m = bsz * seq
    assert d == D_MODEL and seq % CHUNK == 0
    tm = min(m, 1024)
    t_ffn = min(seq, 512)

    w_main, w_gate, hp, dsk, gbias = _prep_weights(w_in, ssd_dt_bias, ssd_a_log, ssd_d,
                                                   mlstm_i_bias, mlstm_f_bias)
    w_out_b = w_out.astype(BF16)
    w_up_b = w_up.astype(BF16)
    w_down_b = w_down.astype(BF16)

    row3 = lambda p: p.reshape(depth, 1, -1)
    ssd_cb, ssd_nw, ml_nw, n2w, ffn_cb = (row3(p) for p in (ssd_conv_b, ssd_norm_w, mlstm_norm_w,
                                                            norm2_w, ffn_conv_b))

    xf = x.reshape(m, d)
    h = _rmsnorm(xf, norm1_w[0], tm)
    for l in range(depth):
        proj, gates = _inproj(h, w_main, w_gate, l, tm, 1024)
        proj3 = proj.reshape(bsz, seq, N_MAIN)
        gates3 = gates.reshape(bsz, seq, N_GATE)
        y_ssd = _ssd_mixer(proj3, gates3, ssd_conv_w, ssd_cb, hp, dsk, ssd_nw, l)
        y_ml = _mlstm_mixer(proj3, gates3, gbias, ml_nw, l)
        x1, h2 = _outproj(xf, y_ssd.reshape(m, SSD_WIDTH), y_ml.reshape(m, ML_WIDTH), w_out_b, n2w, l,
                          min(m, 512))
        last = l == depth - 1
        nw = norm_f_w if last else norm1_w[l + 1]
        xf, h = _ffn(x1, h2, w_up_b, ffn_conv_w, ffn_cb, w_down_b, nw.reshape(1, d), l, seq, t_ffn,
                     512, F32 if last else BF16)
    return h.reshape(bsz, seq, d)
```

```python
import functools

import jax
import jax.numpy as jnp
import numpy as np
from jax import lax
from jax.experimental import pallas as pl
from jax.experimental.pallas import tpu as pltpu

EPS = 1e-6
NEG = -1e30
F32 = jnp.float32
BF16 = jnp.bfloat16

D_MODEL = 2048
SSD_WIDTH = 2048
SSD_HEAD_DIM = 64
SSD_HEADS = 32
SSD_GROUPS = 4
SSD_STATE = 128
SSD_HEADS_PER_GROUP = SSD_HEADS // SSD_GROUPS
SSD_GROUP_WIDTH = SSD_WIDTH // SSD_GROUPS
SSD_CONV = 5
ML_HEADS = 8
ML_DK = 128
ML_DV = 256
ML_WIDTH = ML_HEADS * ML_DV
ML_QK_SCALE = ML_DK ** -0.5
D_FF = 5632
FFN_CONV = 3
CHUNK = 128
LANES = 128
BF16_ROWS = 16
CONV_ROWS = 64
CONV_PAD = 32
CONV_WIN = CONV_ROWS + 2 * CONV_PAD
N_MAIN = 2 * SSD_WIDTH + 2 * SSD_GROUPS * SSD_STATE + 2 * ML_HEADS * ML_DK + 2 * ML_WIDTH
N_GATE = (SSD_GROUPS + 1) * LANES
VMEM_LIMIT = 56 * 1024 * 1024


def _params(*sem):
    return pltpu.CompilerParams(dimension_semantics=sem, vmem_limit_bytes=VMEM_LIMIT)


def _sigmoid(x):
    return 0.5 + 0.5 * jnp.tanh(0.5 * x)


def _silu(x):
    hx = 0.5 * x
    return hx + hx * jnp.tanh(hx)


def _softplus(x):
    return jnp.maximum(x, 0.0) + jnp.log(1.0 + jnp.exp(-jnp.abs(x)))


def _dot(a, b):
    return jnp.dot(a, b, preferred_element_type=F32)


def _dot_nt(a, b):
    return lax.dot_general(a, b, (((1,), (1,)), ((), ())), preferred_element_type=F32)


def _split3(x):
    hi = x.astype(BF16)
    r = x - hi.astype(F32)
    mid = r.astype(BF16)
    lo = (r - mid.astype(F32)).astype(BF16)
    return hi, mid, lo


def _cumdot(tri3, x):
    return _dot(tri3, jnp.concatenate(_split3(x), axis=0))


def _tri3(fwd):
    row = lax.broadcasted_iota(jnp.int32, (CHUNK, 3 * CHUNK), 0)
    col = lax.broadcasted_iota(jnp.int32, (CHUNK, 3 * CHUNK), 1) % CHUNK
    keep = (col <= row) if fwd else (col >= row)
    return jnp.where(keep, 1.0, 0.0).astype(BF16)


def _lane_col(x, lane_iota, lane):
    return jnp.sum(jnp.where(lane_iota == lane, x, 0.0), axis=1, keepdims=True)


def _rmsnorm_kernel(x_ref, w_ref, o_ref):
    x = x_ref[...]
    ms = jnp.mean(x * x, axis=1, keepdims=True)
    o_ref[...] = (x * lax.rsqrt(ms + EPS) * w_ref[...]).astype(o_ref.dtype)


def _rmsnorm(x, w, tm):
    m, d = x.shape
    return pl.pallas_call(
        _rmsnorm_kernel,
        grid=(m // tm,),
        in_specs=[pl.BlockSpec((tm, d), lambda i: (i, 0)), pl.BlockSpec((1, d), lambda i: (0, 0))],
        out_specs=pl.BlockSpec((tm, d), lambda i: (i, 0)),
        out_shape=jax.ShapeDtypeStruct((m, d), BF16),
        compiler_params=_params("parallel"),
        name="rmsnorm_in",
    )(x, w.reshape(1, d))


def _inproj_kernel(na, h_ref, wa_ref, wb_ref, wg_ref, p_ref, g_ref):
    h = h_ref[...]
    j = pl.program_id(1)

    @pl.when(j < na)
    def _():
        p_ref[...] = _dot(h, wa_ref[...]).astype(p_ref.dtype)

    @pl.when(j >= na)
    def _():
        p_ref[...] = _dot(h, wb_ref[...]).astype(p_ref.dtype)

    @pl.when(j == 0)
    def _():
        g_ref[...] = _dot(h, wg_ref[...])


def _inproj(h, w_a, w_b, w_gate, layer, tm, tn):
    m, d = h.shape
    na = w_a.shape[-1] // tn
    return pl.pallas_call(
        functools.partial(_inproj_kernel, na),
        grid=(m // tm, N_MAIN // tn),
        in_specs=[pl.BlockSpec((tm, d), lambda i, j: (i, 0)),
                  pl.BlockSpec((None, d, tn), lambda i, j: (layer, 0, jnp.minimum(j, na - 1))),
                  pl.BlockSpec((None, d, tn), lambda i, j: (layer, 0, jnp.maximum(j - na, 0))),
                  pl.BlockSpec((None, d, N_GATE), lambda i, j: (layer, 0, 0))],
        out_specs=[pl.BlockSpec((tm, tn), lambda i, j: (i, j)),
                   pl.BlockSpec((tm, N_GATE), lambda i, j: (i, 0))],
        out_shape=[jax.ShapeDtypeStruct((m, N_MAIN), BF16),
                   jax.ShapeDtypeStruct((m, N_GATE), F32)],
        compiler_params=_params("parallel", "arbitrary"),
        name="in_proj",
    )(h, w_a, w_b, w_gate)


def _ssd_chunk(c, fwd, refs, consts):
    (xb_ref, bt_ref, cc_ref, acs_ref, rt_ref, y_ref, s_ref) = refs
    (mask, lane_iota, left) = consts
    rows = pl.ds(pl.multiple_of(c * CHUNK, CHUNK), CHUNK)
    acs = acs_ref[rows, :]
    r_t = rt_ref[c]
    cc = cc_ref[rows, :]
    b_t = bt_ref[c]
    scores = _dot(cc, b_t.astype(BF16))
    s_prev = s_ref[...]
    y_off = _dot(cc, s_prev.astype(BF16))
    base = 0 if fwd else SSD_HEADS_PER_GROUP
    edge = CHUNK - 1 if fwd else 0
    for q in range(SSD_HEADS_PER_GROUP // 2):
        cols = slice(q * LANES, (q + 1) * LANES)
        xb = xb_ref[rows, cols]
        zero = jnp.zeros_like(xb)
        xbd = jnp.concatenate([jnp.where(left, xb, zero), jnp.where(left, zero, xb)], axis=0)
        col_a, last, ms, bws = [], [], [], []
        for e in range(2):
            lane = base + 2 * q + e
            ca = _lane_col(acs, lane_iota, lane)
            r_row = r_t[lane:lane + 1, :]
            la = ca[edge:edge + 1, :]
            ms.append((scores * jnp.exp(jnp.where(mask, ca + r_row, NEG))).astype(BF16))
            bws.append((b_t * jnp.exp(r_row + la)).astype(BF16))
            col_a.append(ca)
            last.append(la)
        lhs = jnp.concatenate([jnp.concatenate(ms, axis=1), jnp.concatenate(bws, axis=1)], axis=0)
        res = _dot(lhs, xbd)
        a_sel = jnp.where(left, col_a[0], col_a[1])
        y_ref[rows, cols] = res[:CHUNK] + jnp.exp(a_sel) * y_off[:, cols]
        last_sel = jnp.where(left[0:1, :], last[0], last[1])
        s_ref[:, cols] = jnp.exp(last_sel) * s_prev[:, cols] + res[CHUNK:]


def _ssd_kernel(xs_ref, z_ref, b_ref, c_ref, gt_ref, cwx_ref, cwb_ref, cwc_ref, cbx_ref, cbb_ref,
                cbc_ref, hp_ref, dsk_ref, nw_ref, sh_ref, o_ref,
                stage_ref, xc_ref, xb_ref, bt_ref, cc_ref, acs_ref, rt_ref, yf_ref, yb_ref,
                sf_ref, sb_ref, tri_ref):
    seq = xs_ref.shape[0]
    nc = seq // CHUNK
    rb = min(seq, 256)
    gw = SSD_GROUP_WIDTH
    n = SSD_STATE
    width = gw + 2 * n

    stage_ref[0:CONV_PAD, :] = jnp.zeros((CONV_PAD, width), BF16)
    stage_ref[CONV_PAD + seq:2 * CONV_PAD + seq, :] = jnp.zeros((CONV_PAD, width), BF16)
    for r in range(0, seq, rb):
        dst = slice(CONV_PAD + r, CONV_PAD + r + rb)
        stage_ref[dst, 0:gw] = xs_ref[r:r + rb, :]
        stage_ref[dst, gw:gw + n] = b_ref[r:r + rb, :]
        stage_ref[dst, gw + n:width] = c_ref[r:r + rb, :]

    taps = [k for k in range(SSD_CONV) if k != SSD_CONV // 2]

    cb = CONV_ROWS

    def conv(r, c0, c1, weights, bias):
        win = stage_ref[r:r + CONV_WIN, c0:c1]
        shifted = _dot(sh_ref[...], win)
        centre = stage_ref[CONV_PAD + r:CONV_PAD + r + cb, c0:c1].astype(F32)
        acc = bias + weights[SSD_CONV // 2] * centre
        for i, k in enumerate(taps):
            acc = acc + weights[k] * shifted[i * cb:(i + 1) * cb]
        return _silu(acc)

    half = 2 * LANES
    wbc = [jnp.concatenate([cwb_ref[k:k + 1, :], cwc_ref[k:k + 1, :]], axis=1) for k in range(SSD_CONV)]
    bbc = jnp.concatenate([cbb_ref[...], cbc_ref[...]], axis=1)
    for c in range(nc):
        bcs = []
        for r in range(c * CHUNK, (c + 1) * CHUNK, cb):
            for c0 in range(0, gw, half):
                wx = [cwx_ref[k:k + 1, c0:c0 + half] for k in range(SSD_CONV)]
                xc = conv(r, c0, c0 + half, wx, cbx_ref[:, c0:c0 + half])
                xc_ref[r:r + cb, c0:c0 + half] = xc
                xb_ref[r:r + cb, c0:c0 + half] = xc.astype(BF16)
            bc = conv(r, gw, width, wbc, bbc)
            cc_ref[r:r + cb, :] = bc[:, n:].astype(BF16)
            bcs.append(bc[:, :n])
        bt_ref[c] = jnp.concatenate(bcs, axis=0).T

    row = lax.broadcasted_iota(jnp.int32, (CHUNK, CHUNK), 0)
    col = lax.broadcasted_iota(jnp.int32, (CHUNK, CHUNK), 1)
    left = col < SSD_HEAD_DIM

    tri_ref[0] = _tri3(True)
    tri_ref[1] = _tri3(False)

    def cumsum_body(c, carry):
        rows = pl.ds(pl.multiple_of(c * CHUNK, CHUNK), CHUNK)
        dt = _softplus(gt_ref[rows, :] + hp_ref[0:1, :])
        a_dt = dt * (-jnp.exp(hp_ref[1:2, :]))
        acs = jnp.where(col < SSD_HEADS_PER_GROUP, _cumdot(tri_ref[0], a_dt),
                        _cumdot(tri_ref[1], a_dt))
        acs_ref[rows, :] = acs
        rt_ref[c] = (jnp.log(dt) - acs).T
        return carry

    lax.fori_loop(0, nc, cumsum_body, 0, unroll=4)

    sf_ref[...] = jnp.zeros_like(sf_ref)
    sb_ref[...] = jnp.zeros_like(sb_ref)

    consts_f = (col <= row, col, left)
    consts_b = (col >= row, col, left)
    refs_f = (xb_ref, bt_ref, cc_ref, acs_ref, rt_ref, yf_ref, sf_ref)
    refs_b = (xb_ref, bt_ref, cc_ref, acs_ref, rt_ref, yb_ref, sb_ref)

    def body(j, carry):
        _ssd_chunk(j, True, refs_f, consts_f)
        _ssd_chunk(nc - 1 - j, False, refs_b, consts_b)
        return carry

    lax.fori_loop(0, nc, body, 0, unroll=2)

    for r in range(0, seq, rb):
        rows = slice(r, r + rb)
        y = yf_ref[rows, :] + yb_ref[rows, :] + xc_ref[rows, :] * dsk_ref[...]
        y = y * _silu(z_ref[rows, :].astype(F32))
        ms = jnp.mean(y * y, axis=1, keepdims=True)
        o_ref[rows, :] = (y * lax.rsqrt(ms + EPS) * nw_ref[...]).astype(o_ref.dtype)


def _ssd_mixer(proj, gates, cw, cb, hp, dsk, nw, shifts, layer):
    bsz, seq, _ = proj.shape
    gw, n, g = SSD_GROUP_WIDTH, SSD_STATE, SSD_GROUPS
    xs0, z0 = 0, SSD_WIDTH // gw
    b0 = 2 * SSD_WIDTH // n
    c0 = b0 + g
    cwb0 = SSD_WIDTH // n
    in_specs = [
        pl.BlockSpec((None, seq, gw), lambda b, i: (b, 0, xs0 + i)),
        pl.BlockSpec((None, seq, gw), lambda b, i: (b, 0, z0 + i)),
        pl.BlockSpec((None, seq, n), lambda b, i: (b, 0, b0 + i)),
        pl.BlockSpec((None, seq, n), lambda b, i: (b, 0, c0 + i)),
        pl.BlockSpec((None, seq, LANES), lambda b, i: (b, 0, i)),
        pl.BlockSpec((None, SSD_CONV, gw), lambda b, i: (layer, 0, i)),
        pl.BlockSpec((None, SSD_CONV, n), lambda b, i: (layer, 0, cwb0 + i)),
        pl.BlockSpec((None, SSD_CONV, n), lambda b, i: (layer, 0, cwb0 + g + i)),
        pl.BlockSpec((None, 1, gw), lambda b, i: (layer, 0, i)),
        pl.BlockSpec((None, 1, n), lambda b, i: (layer, 0, cwb0 + i)),
        pl.BlockSpec((None, 1, n), lambda b, i: (layer, 0, cwb0 + g + i)),
        pl.BlockSpec((None, None, 8, LANES), lambda b, i: (layer, i, 0, 0)),
        pl.BlockSpec((None, 1, gw), lambda b, i: (layer, 0, i)),
        pl.BlockSpec((None, 1, gw), lambda b, i: (layer, 0, i)),
        pl.BlockSpec(shifts.shape, lambda b, i: (0, 0)),
    ]
    scratch = [
        pltpu.VMEM((seq + 2 * CONV_PAD, gw + 2 * n), BF16),
        pltpu.VMEM((seq, gw), F32),
        pltpu.VMEM((seq, gw), BF16),
        pltpu.VMEM((seq // CHUNK, n, CHUNK), F32),
        pltpu.VMEM((seq, n), BF16),
        pltpu.VMEM((seq, LANES), F32),
        pltpu.VMEM((seq // CHUNK, LANES, CHUNK), F32),
        pltpu.VMEM((seq, gw), F32),
        pltpu.VMEM((seq, gw), F32),
        pltpu.VMEM((n, gw), F32),
        pltpu.VMEM((n, gw), F32),
        pltpu.VMEM((2, CHUNK, 3 * CHUNK), BF16),
    ]
    return pl.pallas_call(
        _ssd_kernel,
        grid=(bsz, g),
        in_specs=in_specs,
        out_specs=pl.BlockSpec((None, seq, gw), lambda b, i: (b, 0, i)),
        out_shape=jax.ShapeDtypeStruct((bsz, seq, SSD_WIDTH), BF16),
        scratch_shapes=scratch,
        compiler_params=_params("parallel", "parallel"),
        name="ssd_mixer",
    )(proj, proj, proj, proj, gates, cw, cw, cw, cb, cb, cb, hp, dsk, nw, shifts)


def _mlstm_gates(c, fwd, head, m_state, refs, consts):
    (q_ref, k_t_ref, gi_t_ref, bs_ref, bs_t_ref, sb_ref, wi_ref, en_ref, kw_ref, dec_ref) = refs
    (mask, lane_iota) = consts
    nc = dec_ref.shape[0] // 2
    rows = pl.ds(pl.multiple_of(c * CHUNK, CHUNK), CHUNK)
    d = 0 if fwd else 1
    edge = CHUNK - 1 if fwd else 0
    f_lane = (2 + d) * ML_HEADS + head
    li_row = gi_t_ref[c, pl.ds(d * ML_HEADS + head, 1), :]
    b_row = bs_t_ref[d, c, pl.ds(f_lane, 1), :]
    d_row = li_row - b_row
    b_col = _lane_col(bs_ref[d, rows, :], lane_iota, f_lane)
    dm = jnp.where(mask, d_row, NEG)
    g = jnp.maximum(jnp.max(dm, axis=1, keepdims=True), m_state)
    k_t = k_t_ref[c]
    s = _dot(q_ref[rows, :], k_t) * (jnp.exp(dm - g) * ML_QK_SCALE)
    sb_ref[d, rows, :] = s.astype(BF16)
    wi_ref[d, rows, :] = jnp.broadcast_to(jnp.exp(m_state - g) * ML_QK_SCALE, (CHUNK, LANES))
    en_ref[d, rows, :] = jnp.broadcast_to(jnp.exp(-(b_col + g)), (CHUNK, LANES))
    g_last = g[edge:edge + 1, :]
    kw_ref[d, c] = (k_t.astype(F32) * jnp.exp(d_row - g_last)).astype(BF16)
    dec_ref[pl.ds(d * nc + c, 1), :] = jnp.broadcast_to(jnp.exp(m_state - g_last),
                                                        (1, dec_ref.shape[1]))
    return b_col[edge:edge + 1, :] + g_last


def _mlstm_chunk(c, fwd, ones, refs):
    (q_ref, v_ref, sb_ref, wi_ref, en_ref, kw_ref, dec_ref, h_ref, cs_ref) = refs
    nc = dec_ref.shape[0] // 2
    rows = pl.ds(pl.multiple_of(c * CHUNK, CHUNK), CHUNK)
    d = 0 if fwd else 1
    v_aug = jnp.concatenate([v_ref[rows, :], ones], axis=1)
    c_aug = cs_ref[...]
    wi = wi_ref[d, rows, :]
    both = _dot(jnp.concatenate([sb_ref[d, rows, :], kw_ref[d, c]], axis=0), v_aug)
    na = (both[:CHUNK]
          + jnp.concatenate([wi, wi, wi], axis=1) * _dot(q_ref[rows, :], c_aug.astype(BF16)))
    r = 1.0 / jnp.maximum(jnp.abs(na[:, ML_DV:]), en_ref[d, rows, :])
    h_ref[rows, :] = na[:, :ML_DV] * jnp.concatenate([r, r], axis=1)
    cs_ref[...] = dec_ref[pl.ds(d * nc + c, 1), :] * c_aug + both[CHUNK:]


def _mlstm_kernel(q_ref, k_ref, v_ref, o_ref, gt_ref, gb_ref, nw_ref, y_ref,
                  gi_t_ref, bs_ref, bs_t_ref, k_t_ref, sb_ref, wi_ref, en_ref, kw_ref, dec_ref,
                  hf_ref, hb_ref, cf_ref, cb_ref, tri_ref):
    seq = q_ref.shape[0]
    nc = seq // CHUNK
    rb = min(seq, 256)
    head = pl.program_id(1)

    @pl.when(head == 0)
    def _():
        tri_ref[0] = _tri3(True)
        tri_ref[1] = _tri3(False)

        def cumsum_body(c, carry):
            rows = pl.ds(pl.multiple_of(c * CHUNK, CHUNK), CHUNK)
            g = gt_ref[rows, :] + gb_ref[...]
            gi_t_ref[c] = g.T
            lf = -_softplus(-g)
            b_f = _cumdot(tri_ref[0], lf)
            b_b = _cumdot(tri_ref[1], lf)
            bs_ref[0, rows, :] = b_f
            bs_ref[1, rows, :] = b_b
            bs_t_ref[0, c] = b_f.T
            bs_t_ref[1, c] = b_b.T
            return carry

        lax.fori_loop(0, nc, cumsum_body, 0, unroll=2)

    def transpose_body(c, carry):
        rows = pl.ds(pl.multiple_of(c * CHUNK, CHUNK), CHUNK)
        k_t_ref[c] = k_ref[rows, :].astype(F32).T.astype(BF16)
        return carry

    lax.fori_loop(0, nc, transpose_body, 0, unroll=4)
    cf_ref[...] = jnp.zeros_like(cf_ref)
    cb_ref[...] = jnp.zeros_like(cb_ref)

    row = lax.broadcasted_iota(jnp.int32, (CHUNK, CHUNK), 0)
    col = lax.broadcasted_iota(jnp.int32, (CHUNK, CHUNK), 1)
    gate_refs = (q_ref, k_t_ref, gi_t_ref, bs_ref, bs_t_ref, sb_ref, wi_ref, en_ref, kw_ref, dec_ref)

    def gate_body(j, carry):
        m_f = _mlstm_gates(j, True, head, carry[0], gate_refs, (col <= row, col))
        m_b = _mlstm_gates(nc - 1 - j, False, head, carry[1], gate_refs, (col >= row, col))
        return (m_f, m_b)

    m_init = jnp.full((1, 1), NEG, F32)
    lax.fori_loop(0, nc, gate_body, (m_init, m_init), unroll=4)

    ones = jnp.ones((CHUNK, LANES), BF16)
    shared = (q_ref, v_ref, sb_ref, wi_ref, en_ref, kw_ref, dec_ref)

    def body(j, carry):
        _mlstm_chunk(j, True, ones, shared + (hf_ref, cf_ref))
        _mlstm_chunk(nc - 1 - j, False, ones, shared + (hb_ref, cb_ref))
        return carry

    lax.fori_loop(0, nc, body, 0, unroll=2)

    for r in range(0, seq, rb):
        rows = slice(r, r + rb)
        hs = hf_ref[rows, :] + hb_ref[rows, :]
        ms = jnp.mean(hs * hs, axis=1, keepdims=True)
        hn = hs * lax.rsqrt(ms + EPS) * nw_ref[...]
        gate = _sigmoid(o_ref[rows, :].astype(F32))
        y_ref[rows, :] = (gate * hn).astype(y_ref.dtype)


def _mlstm_mixer(proj, gates, gbias, nw, layer):
    bsz, seq, _ = proj.shape
    q0 = (2 * SSD_WIDTH + 2 * SSD_GROUPS * SSD_STATE) // ML_DK
    k0 = q0 + ML_HEADS
    v0 = (2 * SSD_WIDTH + 2 * SSD_GROUPS * SSD_STATE + 2 * ML_HEADS * ML_DK) // ML_DV
    o0 = v0 + ML_HEADS
    in_specs = [
        pl.BlockSpec((None, seq, ML_DK), lambda b, h: (b, 0, q0 + h)),
        pl.BlockSpec((None, seq, ML_DK), lambda b, h: (b, 0, k0 + h)),
        pl.BlockSpec((None, seq, ML_DV), lambda b, h: (b, 0, v0 + h)),
        pl.BlockSpec((None, seq, ML_DV), lambda b, h: (b, 0, o0 + h)),
        pl.BlockSpec((None, seq, LANES), lambda b, h: (b, 0, SSD_GROUPS)),
        pl.BlockSpec((None, 1, LANES), lambda b, h: (layer, 0, 0)),
        pl.BlockSpec((None, 1, ML_DV), lambda b, h: (layer, 0, h)),
    ]
    nc = seq // CHUNK
    aug = ML_DV + LANES
    scratch = [
        pltpu.VMEM((nc, LANES, CHUNK), F32),
        pltpu.VMEM((2, seq, LANES), F32),
        pltpu.VMEM((2, nc, LANES, CHUNK), F32),
        pltpu.VMEM((nc, ML_DK, CHUNK), BF16),
        pltpu.VMEM((2, seq, CHUNK), BF16),
        pltpu.VMEM((2, seq, LANES), F32),
        pltpu.VMEM((2, seq, LANES), F32),
        pltpu.VMEM((2, nc, ML_DK, CHUNK), BF16),
        pltpu.VMEM((2 * nc, aug), F32),
        pltpu.VMEM((seq, ML_DV), F32),
        pltpu.VMEM((seq, ML_DV), F32),
        pltpu.VMEM((ML_DK, aug), F32),
        pltpu.VMEM((ML_DK, aug), F32),
        pltpu.VMEM((2, CHUNK, 3 * CHUNK), BF16),
    ]
    return pl.pallas_call(
        _mlstm_kernel,
        grid=(bsz, ML_HEADS),
        in_specs=in_specs,
        out_specs=pl.BlockSpec((None, seq, ML_DV), lambda b, h: (b, 0, h)),
        out_shape=jax.ShapeDtypeStruct((bsz, seq, ML_WIDTH), BF16),
        scratch_shapes=scratch,
        compiler_params=_params("parallel", "arbitrary"),
        name="mlstm_mixer",
    )(proj, proj, proj, proj, gates, gbias, nw)


def _outproj_kernel(x_ref, ys_ref, ym_ref, ws_ref, wm_ref, nw_ref, x1_ref, h_ref):
    x1 = x_ref[...] + _dot(ys_ref[...], ws_ref[...]) + _dot(ym_ref[...], wm_ref[...])
    x1_ref[...] = x1
    ms = jnp.mean(x1 * x1, axis=1, keepdims=True)
    h_ref[...] = (x1 * lax.rsqrt(ms + EPS) * nw_ref[...]).astype(h_ref.dtype)


def _outproj(x, y_ssd, y_ml, w_out, nw, layer, tm):
    m, d = x.shape
    return pl.pallas_call(
        _outproj_kernel,
        grid=(m // tm,),
        in_specs=[pl.BlockSpec((tm, d), lambda i: (i, 0)),
                  pl.BlockSpec((tm, SSD_WIDTH), lambda i: (i, 0)),
                  pl.BlockSpec((tm, ML_WIDTH), lambda i: (i, 0)),
                  pl.BlockSpec((None, SSD_WIDTH, d), lambda i: (layer, 0, 0),
                               pipeline_mode=pl.Buffered(1)),
                  pl.BlockSpec((None, ML_WIDTH, d), lambda i: (layer, 1, 0),
                               pipeline_mode=pl.Buffered(1)),
                  pl.BlockSpec((None, 1, d), lambda i: (layer, 0, 0))],
        out_specs=[pl.BlockSpec((tm, d), lambda i: (i, 0)),
                   pl.BlockSpec((tm, d), lambda i: (i, 0))],
        out_shape=[jax.ShapeDtypeStruct((m, d), F32), jax.ShapeDtypeStruct((m, d), BF16)],
        compiler_params=_params("parallel"),
        name="out_proj",
    )(x, y_ssd, y_ml, w_out, w_out, nw)


def _ffn_kernel(tiles_per_seq, h_ref, hp_ref, hn_ref, x_ref, wg_ref, wv_ref, cwg_ref, cwv_ref,
                cbg_ref, cbv_ref, wd_ref, nw_ref, x2_ref, o_ref, hext_ref, ug_ref, uv_ref, acc_ref):
    i = pl.program_id(0)
    j = pl.program_id(1)
    t = h_ref.shape[0]
    halo = BF16_ROWS

    @pl.when(j == 0)
    def _():
        pos = i % tiles_per_seq
        prev = hp_ref[...]
        nxt = hn_ref[...]
        hext_ref[0:halo, :] = jnp.where(pos == 0, jnp.zeros_like(prev), prev)
        hext_ref[halo:halo + t, :] = h_ref[...]
        hext_ref[halo + t:2 * halo + t, :] = jnp.where(pos == tiles_per_seq - 1, jnp.zeros_like(nxt), nxt)
        acc_ref[...] = jnp.zeros_like(acc_ref)

    hext = hext_ref[...]
    ug_ref[...] = _dot(hext, wg_ref[...])
    uv_ref[...] = _dot(hext, wv_ref[...])

    def conv(u_ref, w_ref, b_ref):
        acc = b_ref[...] + w_ref[0:1, :] * u_ref[halo - 1:halo - 1 + t, :]
        acc = acc + w_ref[1:2, :] * u_ref[halo:halo + t, :]
        return acc + w_ref[2:3, :] * u_ref[halo + 1:halo + 1 + t, :]

    act = _silu(conv(ug_ref, cwg_ref, cbg_ref)) * conv(uv_ref, cwv_ref, cbv_ref)
    acc_ref[...] += _dot(act.astype(BF16), wd_ref[...])

    @pl.when(j == pl.num_programs(1) - 1)
    def _():
        x2 = x_ref[...] + acc_ref[...]
        x2_ref[...] = x2
        ms = jnp.mean(x2 * x2, axis=1, keepdims=True)
        o_ref[...] = (x2 * lax.rsqrt(ms + EPS) * nw_ref[...]).astype(o_ref.dtype)


def _ffn(x1, h2, w_up, cw, cb, w_down, nw, layer, seq, t, f, out_dtype):
    m, d = x1.shape
    nj = D_FF // f
    tiles_per_seq = seq // t
    hb = t // BF16_ROWS
    last_blk = m // BF16_ROWS - 1
    in_specs = [
        pl.BlockSpec((t, d), lambda i, j: (i, 0)),
        pl.BlockSpec((BF16_ROWS, d), lambda i, j: (jnp.maximum(i * hb - 1, 0), 0)),
        pl.BlockSpec((BF16_ROWS, d), lambda i, j: (jnp.minimum((i + 1) * hb, last_blk), 0)),
        pl.BlockSpec((t, d), lambda i, j: (i, 0)),
        pl.BlockSpec((None, d, f), lambda i, j: (layer, 0, j)),
        pl.BlockSpec((None, d, f), lambda i, j: (layer, 0, nj + j)),
        pl.BlockSpec((None, FFN_CONV, f), lambda i, j: (layer, 0, j)),
        pl.BlockSpec((None, FFN_CONV, f), lambda i, j: (layer, 0, nj + j)),
        pl.BlockSpec((None, 1, f), lambda i, j: (layer, 0, j)),
        pl.BlockSpec((None, 1, f), lambda i, j: (layer, 0, nj + j)),
        pl.BlockSpec((None, f, d), lambda i, j: (layer, j, 0)),
        pl.BlockSpec((1, d), lambda i, j: (0, 0)),
    ]
    scratch = [
        pltpu.VMEM((t + 2 * BF16_ROWS, d), BF16),
        pltpu.VMEM((t + 2 * BF16_ROWS, f), F32),
        pltpu.VMEM((t + 2 * BF16_ROWS, f), F32),
        pltpu.VMEM((t, d), F32),
    ]
    return pl.pallas_call(
        functools.partial(_ffn_kernel, tiles_per_seq),
        grid=(m // t, nj),
        in_specs=in_specs,
        out_specs=[pl.BlockSpec((t, d), lambda i, j: (i, 0)),
                   pl.BlockSpec((t, d), lambda i, j: (i, 0))],
        out_shape=[jax.ShapeDtypeStruct((m, d), F32), jax.ShapeDtypeStruct((m, d), out_dtype)],
        scratch_shapes=scratch,
        compiler_params=_params("parallel", "arbitrary"),
        name="conv_ffn",
    )(h2, h2, h2, x1, w_up, w_up, cw, cw, cb, cb, w_down, nw)


def _pad_lanes(a, width):
    return jnp.pad(a, [(0, 0)] * (a.ndim - 1) + [(0, width - a.shape[-1])])


def _prep_weights(w_in, ssd_dt_bias, ssd_a_log, ssd_d, mlstm_i_bias, mlstm_f_bias):
    depth = w_in.shape[0]
    sizes = (SSD_WIDTH, SSD_WIDTH, SSD_GROUPS * SSD_STATE, SSD_GROUPS * SSD_STATE, 2 * SSD_HEADS,
             ML_HEADS * ML_DK, ML_HEADS * ML_DK, ML_WIDTH, ML_WIDTH, 2 * ML_HEADS, 2 * ML_HEADS)
    offs = np.cumsum((0,) + sizes)
    part = lambda n: w_in[:, :, offs[n]:offs[n + 1]]
    dt, ig, fg = part(4), part(9), part(10)
    n_a = 2 * SSD_WIDTH + 2 * SSD_GROUPS * SSD_STATE
    w_a = w_in[:, :, :n_a].astype(BF16)
    w_b = w_in[:, :, offs[5]:offs[9]].astype(BF16)
    hg = SSD_HEADS_PER_GROUP
    dt = dt.reshape(depth, D_MODEL, 2, SSD_GROUPS, hg)
    gate_blocks = [_pad_lanes(jnp.concatenate([dt[:, :, 0, g], dt[:, :, 1, g]], axis=-1), LANES)
                   for g in range(SSD_GROUPS)]
    gate_blocks.append(_pad_lanes(jnp.concatenate([ig, fg], axis=-1), LANES))
    w_gate = jnp.concatenate(gate_blocks, axis=-1).astype(BF16)
    per_group = lambda p: _pad_lanes(
        jnp.swapaxes(p.reshape(depth, 2, SSD_GROUPS, hg), 1, 2).reshape(depth, SSD_GROUPS, 2 * hg), LANES)
    hp = jnp.stack([per_group(ssd_dt_bias), per_group(ssd_a_log)], axis=2)
    hp = jnp.pad(hp, ((0, 0), (0, 0), (0, 6), (0, 0)))
    dsk = jnp.repeat(ssd_d, SSD_HEAD_DIM, axis=-1).reshape(depth, 1, SSD_WIDTH)
    gbias = _pad_lanes(jnp.concatenate([mlstm_i_bias.reshape(depth, 2 * ML_HEADS),
                                        mlstm_f_bias.reshape(depth, 2 * ML_HEADS)], axis=-1),
                       LANES).reshape(depth, 1, LANES)
    return w_a, w_b, w_gate, hp, dsk, gbias


def _conv_shift_matrices():
    taps = [k for k in range(SSD_CONV) if k != SSD_CONV // 2]
    sh = np.zeros((len(taps) * CONV_ROWS, CONV_WIN), np.float32)
    for i, k in enumerate(taps):
        sh[i * CONV_ROWS + np.arange(CONV_ROWS), np.arange(CONV_ROWS) + k - SSD_CONV // 2 + CONV_PAD] = 1.0
    return jnp.asarray(sh, BF16)


def kernel(x, norm1_w, w_in, ssd_conv_w, ssd_conv_b, ssd_dt_bias, ssd_a_log, ssd_d, ssd_norm_w,
           mlstm_i_bias, mlstm_f_bias, mlstm_norm_w, w_out, norm2_w, w_up, ffn_conv_w, ffn_conv_b,
           w_down, norm_f_w):
    bsz, seq, d = x.shape
    depth = w_in.shape[0]
    m = bsz * seq
    assert d == D_MODEL and seq % CHUNK == 0
    tm = min(m, 1024)
    t_ffn = min(seq, 512)

    w_a, w_b, w_gate, hp, dsk, gbias = _prep_weights(w_in, ssd_dt_bias, ssd_a_log, ssd_d,
                                                   mlstm_i_bias, mlstm_f_bias)
    w_out_b = w_out.astype(BF16)
    w_up_b = w_up.astype(BF16)
    w_down_b = w_down.astype(BF16)

    row3 = lambda p: p.reshape(depth, 1, -1)
    ssd_cb, ssd_nw, ml_nw, n2w, ffn_cb = (row3(p) for p in (ssd_conv_b, ssd_norm_w, mlstm_norm_w,
                                                            norm2_w, ffn_conv_b))

    shifts = _conv_shift_matrices()

    xf = x.reshape(m, d)
    h = _rmsnorm(xf, norm1_w[0], tm)
    for l in range(depth):
        proj, gates = _inproj(h, w_a, w_b, w_gate, l, tm, 1024)
        proj3 = proj.reshape(bsz, seq, N_MAIN)
        gates3 = gates.reshape(bsz, seq, N_GATE)
        y_ssd = _ssd_mixer(proj3, gates3, ssd_conv_w, ssd_cb, hp, dsk, ssd_nw, shifts, l)
        y_ml = _mlstm_mixer(proj3, gates3, gbias, ml_nw, l)
        x1, h2 = _outproj(xf, y_ssd.reshape(m, SSD_WIDTH), y_ml.reshape(m, ML_WIDTH), w_out_b, n2w, l,
                          min(m, 512))
        last = l == depth - 1
        nw = norm_f_w if last else norm1_w[l + 1]
        xf, h = _ffn(x1, h2, w_up_b, ffn_conv_w, ffn_cb, w_down_b, nw.reshape(1, d), l, seq, t_ffn,
                     512, F32 if last else BF16)
    return h.reshape(bsz, seq, d)
```

```python
import functools

import jax
import jax.numpy as jnp
import numpy as np
from jax import lax
from jax.experimental import pallas as pl
from jax.experimental.pallas import tpu as pltpu

EPS = 1e-6
NEG = -1e30
LOG2E = 1.4426950408889634
F32 = jnp.float32
BF16 = jnp.bfloat16

D_MODEL = 2048
SSD_WIDTH = 2048
SSD_HEAD_DIM = 64
SSD_HEADS = 32
SSD_GROUPS = 4
SSD_STATE = 128
SSD_HEADS_PER_GROUP = SSD_HEADS // SSD_GROUPS
SSD_GROUP_WIDTH = SSD_WIDTH // SSD_GROUPS
SSD_CONV = 5
ML_HEADS = 8
ML_DK = 128
ML_DV = 256
ML_WIDTH = ML_HEADS * ML_DV
ML_QK_SCALE = ML_DK ** -0.5
ML_HEADS_PER_STEP = 2
D_FF = 5632
FFN_CONV = 3
CHUNK = 128
LANES = 128
BF16_ROWS = 16
CONV_ROWS = 64
CONV_PAD = 32
CONV_WIN = CONV_ROWS + 2 * CONV_PAD
N_MAIN = 2 * SSD_WIDTH + 2 * SSD_GROUPS * SSD_STATE + 2 * ML_HEADS * ML_DK + 2 * ML_WIDTH
N_GATE = LANES
ML_GATE_LANE0 = 2 * SSD_HEADS
VMEM_LIMIT = 56 * 1024 * 1024


def _params(*sem):
    return pltpu.CompilerParams(dimension_semantics=sem, vmem_limit_bytes=VMEM_LIMIT)


def _sigmoid(x):
    return 0.5 + 0.5 * jnp.tanh(0.5 * x)


def _silu(x):
    hx = 0.5 * x
    return hx + hx * jnp.tanh(hx)


def _softplus(x):
    return jnp.maximum(x, 0.0) + jnp.log(1.0 + jnp.exp(-jnp.abs(x)))


def _dot(a, b):
    return jnp.dot(a, b, preferred_element_type=F32)


def _split3(x):
    hi = x.astype(BF16)
    r = x - hi.astype(F32)
    mid = r.astype(BF16)
    lo = (r - mid.astype(F32)).astype(BF16)
    return hi, mid, lo


def _cumdot(tri3, x):
    return _dot(tri3, jnp.concatenate(_split3(x), axis=0))


def _tri3(fwd):
    row = lax.broadcasted_iota(jnp.int32, (CHUNK, 3 * CHUNK), 0)
    col = lax.broadcasted_iota(jnp.int32, (CHUNK, 3 * CHUNK), 1) % CHUNK
    keep = (col <= row) if fwd else (col >= row)
    return jnp.where(keep, 1.0, 0.0).astype(BF16)


def _lane_col(x, lane_iota, lane):
    return jnp.sum(jnp.where(lane_iota == lane, x, 0.0), axis=1, keepdims=True)


def _rmsnorm_kernel(x_ref, w_ref, o_ref):
    x = x_ref[...]
    ms = jnp.mean(x * x, axis=1, keepdims=True)
    o_ref[...] = (x * lax.rsqrt(ms + EPS) * w_ref[...]).astype(o_ref.dtype)


def _rmsnorm(x, w, tm):
    m, d = x.shape
    return pl.pallas_call(
        _rmsnorm_kernel,
        grid=(m // tm,),
        in_specs=[pl.BlockSpec((tm, d), lambda i: (i, 0)), pl.BlockSpec((1, d), lambda i: (0, 0))],
        out_specs=pl.BlockSpec((tm, d), lambda i: (i, 0)),
        out_shape=jax.ShapeDtypeStruct((m, d), BF16),
        compiler_params=_params("parallel"),
        name="rmsnorm_in",
    )(x, w.reshape(1, d))


def _inproj_kernel(h_ref, w_ref, wg_ref, p_ref, g_ref):
    h = h_ref[...]
    p_ref[...] = _dot(h, w_ref[...]).astype(p_ref.dtype)

    @pl.when(pl.program_id(1) == 0)
    def _():
        g_ref[...] = _dot(h, wg_ref[...])


def _inproj(h, w_main, w_gate, layer, tm, tn):
    m, d = h.shape
    return pl.pallas_call(
        _inproj_kernel,
        grid=(m // tm, N_MAIN // tn),
        in_specs=[pl.BlockSpec((tm, d), lambda i, j: (i, 0)),
                  pl.BlockSpec((None, d, tn), lambda i, j: (layer, 0, j)),
                  pl.BlockSpec((None, d, N_GATE), lambda i, j: (layer, 0, 0))],
        out_specs=[pl.BlockSpec((tm, tn), lambda i, j: (i, j)),
                   pl.BlockSpec((tm, N_GATE), lambda i, j: (i, 0))],
        out_shape=[jax.ShapeDtypeStruct((m, N_MAIN), BF16),
                   jax.ShapeDtypeStruct((m, N_GATE), F32)],
        compiler_params=_params("parallel", "arbitrary"),
        name="in_proj",
    )(h, w_main, w_gate)


def _ssd_chunk(c, fwd, refs, consts):
    (xb_ref, bt_ref, cc_ref, acs_ref, rt_ref, sel_ref, y_ref, s_ref) = refs
    (mask, lane0, left) = consts
    rows = pl.ds(pl.multiple_of(c * CHUNK, CHUNK), CHUNK)
    acs = acs_ref[rows, :]
    cc = cc_ref[rows, :]
    b_t = bt_ref[c]
    scores = _dot(cc, b_t.astype(BF16))
    s_prev = s_ref[...]
    y_off = _dot(cc, s_prev.astype(BF16))
    base = 0 if fwd else SSD_HEADS_PER_GROUP
    edge = CHUNK - 1 if fwd else 0
    for q in range(SSD_HEADS_PER_GROUP // 2):
        cols = slice(q * LANES, (q + 1) * LANES)
        xb = xb_ref[rows, cols]
        zero = jnp.zeros_like(xb)
        xbd = jnp.concatenate([jnp.where(left, xb, zero), jnp.where(left, zero, xb)], axis=0)
        col_a, last, ms, bws = [], [], [], []
        for e in range(2):
            hd = base + 2 * q + e
            ca = jnp.sum(acs * sel_ref[hd], axis=1, keepdims=True)
            r_row = rt_ref[c, pl.ds(lane0 + hd, 1), :]
            la = ca[edge:edge + 1, :]
            ms.append((scores * jnp.exp2(jnp.where(mask, ca + r_row, NEG))).astype(BF16))
            bws.append((b_t * jnp.exp2(r_row + la)).astype(BF16))
            col_a.append(ca)
            last.append(la)
        lhs = jnp.concatenate([jnp.concatenate(ms, axis=1), jnp.concatenate(bws, axis=1)], axis=0)
        res = _dot(lhs, xbd)
        a_sel = jnp.where(left, col_a[0], col_a[1])
        y_ref[rows, cols] = res[:CHUNK] + jnp.exp2(a_sel) * y_off[:, cols]
        last_sel = jnp.where(left[0:1, :], last[0], last[1])
        s_ref[:, cols] = jnp.exp2(last_sel) * s_prev[:, cols] + res[CHUNK:]


def _ssd_kernel(xs_ref, z_ref, b_ref, c_ref, gt_ref, cwx_ref, cwb_ref, cwc_ref, cbx_ref, cbb_ref,
                cbc_ref, hp_ref, dsk_ref, nw_ref, sh_ref, o_ref,
                stage_ref, xc_ref, xb_ref, bt_ref, cc_ref, acs_ref, rt_ref, yf_ref, yb_ref,
                sf_ref, sb_ref, tri_ref, sel_ref):
    seq = xs_ref.shape[0]
    nc = seq // CHUNK
    rb = min(seq, 256)
    gw = SSD_GROUP_WIDTH
    n = SSD_STATE
    width = gw + 2 * n

    stage_ref[0:CONV_PAD, :] = jnp.zeros((CONV_PAD, width), BF16)
    stage_ref[CONV_PAD + seq:2 * CONV_PAD + seq, :] = jnp.zeros((CONV_PAD, width), BF16)
    for r in range(0, seq, rb):
        dst = slice(CONV_PAD + r, CONV_PAD + r + rb)
        stage_ref[dst, 0:gw] = xs_ref[r:r + rb, :]
        stage_ref[dst, gw:gw + n] = b_ref[r:r + rb, :]
        stage_ref[dst, gw + n:width] = c_ref[r:r + rb, :]

    taps = [k for k in range(SSD_CONV) if k != SSD_CONV // 2]

    cb = CONV_ROWS

    def conv(r, c0, c1, weights, bias):
        win = stage_ref[r:r + CONV_WIN, c0:c1]
        shifted = _dot(sh_ref[...], win)
        centre = stage_ref[CONV_PAD + r:CONV_PAD + r + cb, c0:c1].astype(F32)
        acc = bias + weights[SSD_CONV // 2] * centre
        for i, k in enumerate(taps):
            acc = acc + weights[k] * shifted[i * cb:(i + 1) * cb]
        return _silu(acc)

    half = 2 * LANES
    wbc = [jnp.concatenate([cwb_ref[k:k + 1, :], cwc_ref[k:k + 1, :]], axis=1) for k in range(SSD_CONV)]
    bbc = jnp.concatenate([cbb_ref[...], cbc_ref[...]], axis=1)
    for c in range(nc):
        bcs = []
        for r in range(c * CHUNK, (c + 1) * CHUNK, cb):
            for c0 in range(0, gw, half):
                wx = [cwx_ref[k:k + 1, c0:c0 + half] for k in range(SSD_CONV)]
                xc = conv(r, c0, c0 + half, wx, cbx_ref[:, c0:c0 + half])
                xc_ref[r:r + cb, c0:c0 + half] = xc
                xb_ref[r:r + cb, c0:c0 + half] = xc.astype(BF16)
            bc = conv(r, gw, width, wbc, bbc)
            cc_ref[r:r + cb, :] = bc[:, n:].astype(BF16)
            bcs.append(bc[:, :n])
        bt_ref[c] = jnp.concatenate(bcs, axis=0).T

    row = lax.broadcasted_iota(jnp.int32, (CHUNK, CHUNK), 0)
    col = lax.broadcasted_iota(jnp.int32, (CHUNK, CHUNK), 1)
    left = col < SSD_HEAD_DIM

    hg = SSD_HEADS_PER_GROUP
    lane0 = pl.program_id(1) * (2 * hg)
    fwd_lane = (col % (2 * hg)) < hg
    for hd in range(2 * hg):
        sel_ref[hd] = jnp.where(col == lane0 + hd, 1.0, 0.0)
    tri_ref[0] = _tri3(True)
    tri_ref[1] = _tri3(False)

    def cumsum_body(c, carry):
        rows = pl.ds(pl.multiple_of(c * CHUNK, CHUNK), CHUNK)
        dt = _softplus(gt_ref[rows, :] + hp_ref[0:1, :])
        a_dt = dt * (-jnp.exp(hp_ref[1:2, :]))
        acs = jnp.where(fwd_lane, _cumdot(tri_ref[0], a_dt), _cumdot(tri_ref[1], a_dt))
        acs_ref[rows, :] = acs * LOG2E
        rt_ref[c] = ((jnp.log(dt) - acs) * LOG2E).T
        return carry

    lax.fori_loop(0, nc, cumsum_body, 0, unroll=4)

    sf_ref[...] = jnp.zeros_like(sf_ref)
    sb_ref[...] = jnp.zeros_like(sb_ref)

    consts_f = (col <= row, lane0, left)
    consts_b = (col >= row, lane0, left)
    refs_f = (xb_ref, bt_ref, cc_ref, acs_ref, rt_ref, sel_ref, yf_ref, sf_ref)
    refs_b = (xb_ref, bt_ref, cc_ref, acs_ref, rt_ref, sel_ref, yb_ref, sb_ref)

    def body(j, carry):
        _ssd_chunk(j, True, refs_f, consts_f)
        _ssd_chunk(nc - 1 - j, False, refs_b, consts_b)
        return carry

    lax.fori_loop(0, nc, body, 0, unroll=2)

    for r in range(0, seq, rb):
        rows = slice(r, r + rb)
        y = yf_ref[rows, :] + yb_ref[rows, :] + xc_ref[rows, :] * dsk_ref[...]
        y = y * _silu(z_ref[rows, :].astype(F32))
        ms = jnp.mean(y * y, axis=1, keepdims=True)
        o_ref[rows, :] = (y * lax.rsqrt(ms + EPS) * nw_ref[...]).astype(o_ref.dtype)


def _ssd_mixer(proj, gates, cw, cb, hp, dsk, nw, shifts, layer):
    bsz, seq, _ = proj.shape
    gw, n, g = SSD_GROUP_WIDTH, SSD_STATE, SSD_GROUPS
    xs0, z0 = 0, SSD_WIDTH // gw
    b0 = 2 * SSD_WIDTH // n
    c0 = b0 + g
    cwb0 = SSD_WIDTH // n
    in_specs = [
        pl.BlockSpec((None, seq, gw), lambda b, i: (b, 0, xs0 + i)),
        pl.BlockSpec((None, seq, gw), lambda b, i: (b, 0, z0 + i)),
        pl.BlockSpec((None, seq, n), lambda b, i: (b, 0, b0 + i)),
        pl.BlockSpec((None, seq, n), lambda b, i: (b, 0, c0 + i)),
        pl.BlockSpec((None, seq, LANES), lambda b, i: (b, 0, 0)),
        pl.BlockSpec((None, SSD_CONV, gw), lambda b, i: (layer, 0, i)),
        pl.BlockSpec((None, SSD_CONV, n), lambda b, i: (layer, 0, cwb0 + i)),
        pl.BlockSpec((None, SSD_CONV, n), lambda b, i: (layer, 0, cwb0 + g + i)),
        pl.BlockSpec((None, 1, gw), lambda b, i: (layer, 0, i)),
        pl.BlockSpec((None, 1, n), lambda b, i: (layer, 0, cwb0 + i)),
        pl.BlockSpec((None, 1, n), lambda b, i: (layer, 0, cwb0 + g + i)),
        pl.BlockSpec((None, 8, LANES), lambda b, i: (layer, 0, 0)),
        pl.BlockSpec((None, 1, gw), lambda b, i: (layer, 0, i)),
        pl.BlockSpec((None, 1, gw), lambda b, i: (layer, 0, i)),
        pl.BlockSpec(shifts.shape, lambda b, i: (0, 0)),
    ]
    scratch = [
        pltpu.VMEM((seq + 2 * CONV_PAD, gw + 2 * n), BF16),
        pltpu.VMEM((seq, gw), F32),
        pltpu.VMEM((seq, gw), BF16),
        pltpu.VMEM((seq // CHUNK, n, CHUNK), F32),
        pltpu.VMEM((seq, n), BF16),
        pltpu.VMEM((seq, LANES), F32),
        pltpu.VMEM((seq // CHUNK, LANES, CHUNK), F32),
        pltpu.VMEM((seq, gw), F32),
        pltpu.VMEM((seq, gw), F32),
        pltpu.VMEM((n, gw), F32),
        pltpu.VMEM((n, gw), F32),
        pltpu.VMEM((2, CHUNK, 3 * CHUNK), BF16),
        pltpu.VMEM((2 * SSD_HEADS_PER_GROUP, CHUNK, LANES), F32),
    ]
    return pl.pallas_call(
        _ssd_kernel,
        grid=(bsz, g),
        in_specs=in_specs,
        out_specs=pl.BlockSpec((None, seq, gw), lambda b, i: (b, 0, i)),
        out_shape=jax.ShapeDtypeStruct((bsz, seq, SSD_WIDTH), BF16),
        scratch_shapes=scratch,
        compiler_params=_params("parallel", "parallel"),
        name="ssd_mixer",
    )(proj, proj, proj, proj, gates, cw, cw, cw, cb, cb, cb, hp, dsk, nw, shifts)


def _mlstm_gates(c, fwd, head, hh, m_state, refs, consts):
    (q_ref, k_t_ref, gi_t_ref, bs_ref, bs_t_ref, sb_ref, wi_ref, en_ref, kw_ref, dec_ref) = refs
    (mask, lane_iota) = consts
    nc = dec_ref.shape[1] // 2
    rows = pl.ds(pl.multiple_of(c * CHUNK, CHUNK), CHUNK)
    d = 0 if fwd else 1
    edge = CHUNK - 1 if fwd else 0
    f_lane = ML_GATE_LANE0 + (2 + d) * ML_HEADS + head
    li_row = gi_t_ref[c, pl.ds(ML_GATE_LANE0 + d * ML_HEADS + head, 1), :]
    b_row = bs_t_ref[d, c, pl.ds(f_lane, 1), :]
    d_row = li_row - b_row
    b_col = _lane_col(bs_ref[d, rows, :], lane_iota, f_lane)
    dm = jnp.where(mask, d_row, NEG)
    g = jnp.maximum(jnp.max(dm, axis=1, keepdims=True), m_state)
    k_t = k_t_ref[hh, c]
    q = q_ref[rows, hh * ML_DK:(hh + 1) * ML_DK]
    s = _dot(q, k_t) * (jnp.exp(dm - g) * ML_QK_SCALE)
    sb_ref[hh, d, rows, :] = s.astype(BF16)
    wi_ref[hh, d, rows, :] = jnp.broadcast_to(jnp.exp(m_state - g) * ML_QK_SCALE, (CHUNK, LANES))
    en_ref[hh, d, rows, :] = jnp.broadcast_to(jnp.exp(-(b_col + g)), (CHUNK, LANES))
    g_last = g[edge:edge + 1, :]
    kw_ref[hh, d, c] = (k_t.astype(F32) * jnp.exp(d_row - g_last)).astype(BF16)
    dec_ref[hh, pl.ds(d * nc + c, 1), :] = jnp.broadcast_to(jnp.exp(m_state - g_last),
                                                            (1, dec_ref.shape[2]))
    return b_col[edge:edge + 1, :] + g_last


def _mlstm_chunk(c, fwd, hh, ones, refs):
    (q_ref, v_ref, sb_ref, wi_ref, en_ref, kw_ref, dec_ref, h_ref, cs_ref) = refs
    nc = dec_ref.shape[1] // 2
    rows = pl.ds(pl.multiple_of(c * CHUNK, CHUNK), CHUNK)
    d = 0 if fwd else 1
    vcols = slice(hh * ML_DV, (hh + 1) * ML_DV)
    v_aug = jnp.concatenate([v_ref[rows, vcols], ones], axis=1)
    c_aug = cs_ref[hh]
    wi = wi_ref[hh, d, rows, :]
    both = _dot(jnp.concatenate([sb_ref[hh, d, rows, :], kw_ref[hh, d, c]], axis=0), v_aug)
    q = q_ref[rows, hh * ML_DK:(hh + 1) * ML_DK]
    na = both[:CHUNK] + jnp.concatenate([wi, wi, wi], axis=1) * _dot(q, c_aug.astype(BF16))
    r = 1.0 / jnp.maximum(jnp.abs(na[:, ML_DV:]), en_ref[hh, d, rows, :])
    h_ref[rows, vcols] = na[:, :ML_DV] * jnp.concatenate([r, r], axis=1)
    cs_ref[hh] = dec_ref[hh, pl.ds(d * nc + c, 1), :] * c_aug + both[CHUNK:]


def _mlstm_kernel(q_ref, k_ref, v_ref, o_ref, gt_ref, gb_ref, nw_ref, y_ref,
                  gi_t_ref, bs_ref, bs_t_ref, k_t_ref, sb_ref, wi_ref, en_ref, kw_ref, dec_ref,
                  hf_ref, hb_ref, cf_ref, cb_ref, tri_ref):
    seq = q_ref.shape[0]
    nc = seq // CHUNK
    rb = min(seq, 256)
    hps = ML_HEADS_PER_STEP
    head0 = pl.program_id(1) * hps

    @pl.when(head0 == 0)
    def _():
        tri_ref[0] = _tri3(True)
        tri_ref[1] = _tri3(False)

        def cumsum_body(c, carry):
            rows = pl.ds(pl.multiple_of(c * CHUNK, CHUNK), CHUNK)
            g = gt_ref[rows, :] + gb_ref[...]
            gi_t_ref[c] = g.T
            lf = -_softplus(-g)
            b_f = _cumdot(tri_ref[0], lf)
            b_b = _cumdot(tri_ref[1], lf)
            bs_ref[0, rows, :] = b_f
            bs_ref[1, rows, :] = b_b
            bs_t_ref[0, c] = b_f.T
            bs_t_ref[1, c] = b_b.T
            return carry

        lax.fori_loop(0, nc, cumsum_body, 0, unroll=2)

    def transpose_body(c, carry):
        rows = pl.ds(pl.multiple_of(c * CHUNK, CHUNK), CHUNK)
        for hh in range(hps):
            k_t_ref[hh, c] = k_ref[rows, hh * ML_DK:(hh + 1) * ML_DK].astype(F32).T.astype(BF16)
        return carry

    lax.fori_loop(0, nc, transpose_body, 0, unroll=2)
    cf_ref[...] = jnp.zeros_like(cf_ref)
    cb_ref[...] = jnp.zeros_like(cb_ref)

    row = lax.broadcasted_iota(jnp.int32, (CHUNK, CHUNK), 0)
    col = lax.broadcasted_iota(jnp.int32, (CHUNK, CHUNK), 1)
    gate_refs = (q_ref, k_t_ref, gi_t_ref, bs_ref, bs_t_ref, sb_ref, wi_ref, en_ref, kw_ref, dec_ref)

    def gate_body(j, carry):
        out = []
        for hh in range(hps):
            out.append(_mlstm_gates(j, True, head0 + hh, hh, carry[2 * hh], gate_refs,
                                    (col <= row, col)))
            out.append(_mlstm_gates(nc - 1 - j, False, head0 + hh, hh, carry[2 * hh + 1], gate_refs,
                                    (col >= row, col)))
        return tuple(out)

    m_init = jnp.full((1, 1), NEG, F32)
    lax.fori_loop(0, nc, gate_body, (m_init,) * (2 * hps), unroll=2)

    ones = jnp.ones((CHUNK, LANES), BF16)
    shared = (q_ref, v_ref, sb_ref, wi_ref, en_ref, kw_ref, dec_ref)

    def body(j, carry):
        for hh in range(hps):
            _mlstm_chunk(j, True, hh, ones, shared + (hf_ref, cf_ref))
            _mlstm_chunk(nc - 1 - j, False, hh, ones, shared + (hb_ref, cb_ref))
        return carry

    lax.fori_loop(0, nc, body, 0)

    for r in range(0, seq, rb):
        rows = slice(r, r + rb)
        for hh in range(hps):
            vcols = slice(hh * ML_DV, (hh + 1) * ML_DV)
            hs = hf_ref[rows, vcols] + hb_ref[rows, vcols]
            ms = jnp.mean(hs * hs, axis=1, keepdims=True)
            hn = hs * lax.rsqrt(ms + EPS) * nw_ref[:, vcols]
            gate = _sigmoid(o_ref[rows, vcols].astype(F32))
            y_ref[rows, vcols] = (gate * hn).astype(y_ref.dtype)


def _mlstm_mixer(proj, gates, gbias, nw, layer):
    bsz, seq, _ = proj.shape
    hps = ML_HEADS_PER_STEP
    qk_w, v_w = hps * ML_DK, hps * ML_DV
    q_off = 2 * SSD_WIDTH + 2 * SSD_GROUPS * SSD_STATE
    q0 = q_off // qk_w
    k0 = q0 + ML_HEADS // hps
    v0 = (q_off + 2 * ML_HEADS * ML_DK) // v_w
    o0 = v0 + ML_HEADS // hps
    in_specs = [
        pl.BlockSpec((None, seq, qk_w), lambda b, h: (b, 0, q0 + h)),
        pl.BlockSpec((None, seq, qk_w), lambda b, h: (b, 0, k0 + h)),
        pl.BlockSpec((None, seq, v_w), lambda b, h: (b, 0, v0 + h)),
        pl.BlockSpec((None, seq, v_w), lambda b, h: (b, 0, o0 + h)),
        pl.BlockSpec((None, seq, LANES), lambda b, h: (b, 0, 0)),
        pl.BlockSpec((None, 1, LANES), lambda b, h: (layer, 0, 0)),
        pl.BlockSpec((None, 1, v_w), lambda b, h: (layer, 0, h)),
    ]
    nc = seq // CHUNK
    aug = ML_DV + LANES
    scratch = [
        pltpu.VMEM((nc, LANES, CHUNK), F32),
        pltpu.VMEM((2, seq, LANES), F32),
        pltpu.VMEM((2, nc, LANES, CHUNK), F32),
        pltpu.VMEM((hps, nc, ML_DK, CHUNK), BF16),
        pltpu.VMEM((hps, 2, seq, CHUNK), BF16),
        pltpu.VMEM((hps, 2, seq, LANES), F32),
        pltpu.VMEM((hps, 2, seq, LANES), F32),
        pltpu.VMEM((hps, 2, nc, ML_DK, CHUNK), BF16),
        pltpu.VMEM((hps, 2 * nc, aug), F32),
        pltpu.VMEM((seq, v_w), F32),
        pltpu.VMEM((seq, v_w), F32),
        pltpu.VMEM((hps, ML_DK, aug), F32),
        pltpu.VMEM((hps, ML_DK, aug), F32),
        pltpu.VMEM((2, CHUNK, 3 * CHUNK), BF16),
    ]
    return pl.pallas_call(
        _mlstm_kernel,
        grid=(bsz, ML_HEADS // hps),
        in_specs=in_specs,
        out_specs=pl.BlockSpec((None, seq, v_w), lambda b, h: (b, 0, h)),
        out_shape=jax.ShapeDtypeStruct((bsz, seq, ML_WIDTH), BF16),
        scratch_shapes=scratch,
        compiler_params=_params("parallel", "arbitrary"),
        name="mlstm_mixer",
    )(proj, proj, proj, proj, gates, gbias, nw)


def _outproj_kernel(x_ref, ys_ref, ym_ref, ws_ref, wm_ref, nw_ref, x1_ref, h_ref):
    x1 = x_ref[...] + _dot(ys_ref[...], ws_ref[...]) + _dot(ym_ref[...], wm_ref[...])
    x1_ref[...] = x1
    ms = jnp.mean(x1 * x1, axis=1, keepdims=True)
    h_ref[...] = (x1 * lax.rsqrt(ms + EPS) * nw_ref[...]).astype(h_ref.dtype)


def _outproj(x, y_ssd, y_ml, w_out, nw, layer, tm):
    m, d = x.shape
    return pl.pallas_call(
        _outproj_kernel,
        grid=(m // tm,),
        in_specs=[pl.BlockSpec((tm, d), lambda i: (i, 0)),
                  pl.BlockSpec((tm, SSD_WIDTH), lambda i: (i, 0)),
                  pl.BlockSpec((tm, ML_WIDTH), lambda i: (i, 0)),
                  pl.BlockSpec((None, SSD_WIDTH, d), lambda i: (layer, 0, 0),
                               pipeline_mode=pl.Buffered(1)),
                  pl.BlockSpec((None, ML_WIDTH, d), lambda i: (layer, 1, 0),
                               pipeline_mode=pl.Buffered(1)),
                  pl.BlockSpec((None, 1, d), lambda i: (layer, 0, 0))],
        out_specs=[pl.BlockSpec((tm, d), lambda i: (i, 0)),
                   pl.BlockSpec((tm, d), lambda i: (i, 0))],
        out_shape=[jax.ShapeDtypeStruct((m, d), F32), jax.ShapeDtypeStruct((m, d), BF16)],
        compiler_params=_params("parallel"),
        name="out_proj",
    )(x, y_ssd, y_ml, w_out, w_out, nw)


def _ffn_kernel(tiles_per_seq, h_ref, hp_ref, hn_ref, x_ref, wg_ref, wv_ref, cwg_ref, cwv_ref,
                cbg_ref, cbv_ref, wd_ref, nw_ref, x2_ref, o_ref, hext_ref, ug_ref, uv_ref, acc_ref):
    i = pl.program_id(0)
    j = pl.program_id(1)
    t = h_ref.shape[0]
    halo = BF16_ROWS

    @pl.when(j == 0)
    def _():
        pos = i % tiles_per_seq
        prev = hp_ref[...]
        nxt = hn_ref[...]
        hext_ref[0:halo, :] = jnp.where(pos == 0, jnp.zeros_like(prev), prev)
        hext_ref[halo:halo + t, :] = h_ref[...]
        hext_ref[halo + t:2 * halo + t, :] = jnp.where(pos == tiles_per_seq - 1, jnp.zeros_like(nxt), nxt)
        acc_ref[...] = jnp.zeros_like(acc_ref)

    hext = hext_ref[...]
    ug_ref[...] = _dot(hext, wg_ref[...])
    uv_ref[...] = _dot(hext, wv_ref[...])

    def conv(u_ref, w_ref, b_ref):
        acc = b_ref[...] + w_ref[0:1, :] * u_ref[halo - 1:halo - 1 + t, :]
        acc = acc + w_ref[1:2, :] * u_ref[halo:halo + t, :]
        return acc + w_ref[2:3, :] * u_ref[halo + 1:halo + 1 + t, :]

    act = _silu(conv(ug_ref, cwg_ref, cbg_ref)) * conv(uv_ref, cwv_ref, cbv_ref)
    acc_ref[...] += _dot(act.astype(BF16), wd_ref[...])

    @pl.when(j == pl.num_programs(1) - 1)
    def _():
        x2 = x_ref[...] + acc_ref[...]
        x2_ref[...] = x2
        ms = jnp.mean(x2 * x2, axis=1, keepdims=True)
        o_ref[...] = (x2 * lax.rsqrt(ms + EPS) * nw_ref[...]).astype(o_ref.dtype)


def _ffn(x1, h2, w_up, cw, cb, w_down, nw, layer, seq, t, f, out_dtype):
    m, d = x1.shape
    nj = D_FF // f
    tiles_per_seq = seq // t
    hb = t // BF16_ROWS
    last_blk = m // BF16_ROWS - 1
    in_specs = [
        pl.BlockSpec((t, d), lambda i, j: (i, 0)),
        pl.BlockSpec((BF16_ROWS, d), lambda i, j: (jnp.maximum(i * hb - 1, 0), 0)),
        pl.BlockSpec((BF16_ROWS, d), lambda i, j: (jnp.minimum((i + 1) * hb, last_blk), 0)),
        pl.BlockSpec((t, d), lambda i, j: (i, 0)),
        pl.BlockSpec((None, d, f), lambda i, j: (layer, 0, j)),
        pl.BlockSpec((None, d, f), lambda i, j: (layer, 0, nj + j)),
        pl.BlockSpec((None, FFN_CONV, f), lambda i, j: (layer, 0, j)),
        pl.BlockSpec((None, FFN_CONV, f), lambda i, j: (layer, 0, nj + j)),
        pl.BlockSpec((None, 1, f), lambda i, j: (layer, 0, j)),
        pl.BlockSpec((None, 1, f), lambda i, j: (layer, 0, nj + j)),
        pl.BlockSpec((None, f, d), lambda i, j: (layer, j, 0)),
        pl.BlockSpec((1, d), lambda i, j: (0, 0)),
    ]
    scratch = [
        pltpu.VMEM((t + 2 * BF16_ROWS, d), BF16),
        pltpu.VMEM((t + 2 * BF16_ROWS, f), F32),
        pltpu.VMEM((t + 2 * BF16_ROWS, f), F32),
        pltpu.VMEM((t, d), F32),
    ]
    return pl.pallas_call(
        functools.partial(_ffn_kernel, tiles_per_seq),
        grid=(m // t, nj),
        in_specs=in_specs,
        out_specs=[pl.BlockSpec((t, d), lambda i, j: (i, 0)),
                   pl.BlockSpec((t, d), lambda i, j: (i, 0))],
        out_shape=[jax.ShapeDtypeStruct((m, d), F32), jax.ShapeDtypeStruct((m, d), out_dtype)],
        scratch_shapes=scratch,
        compiler_params=_params("parallel", "arbitrary"),
        name="conv_ffn",
    )(h2, h2, h2, x1, w_up, w_up, cw, cw, cb, cb, w_down, nw)


def _pad_lanes(a, width):
    return jnp.pad(a, [(0, 0)] * (a.ndim - 1) + [(0, width - a.shape[-1])])


def _prep_weights(w_in, ssd_dt_bias, ssd_a_log, ssd_d, mlstm_i_bias, mlstm_f_bias):
    depth = w_in.shape[0]
    sizes = (SSD_WIDTH, SSD_WIDTH, SSD_GROUPS * SSD_STATE, SSD_GROUPS * SSD_STATE, 2 * SSD_HEADS,
             ML_HEADS * ML_DK, ML_HEADS * ML_DK, ML_WIDTH, ML_WIDTH, 2 * ML_HEADS, 2 * ML_HEADS)
    offs = np.cumsum((0,) + sizes)
    part = lambda n: w_in[:, :, offs[n]:offs[n + 1]]
    dt, ig, fg = part(4), part(9), part(10)
    n_a = 2 * SSD_WIDTH + 2 * SSD_GROUPS * SSD_STATE
    w_main = jnp.concatenate([w_in[:, :, :n_a].astype(BF16),
                              w_in[:, :, offs[5]:offs[9]].astype(BF16)],
                             axis=-1)
    hg = SSD_HEADS_PER_GROUP
    by_group = lambda p, lead: jnp.swapaxes(p.reshape(lead + (2, SSD_GROUPS, hg)), -3, -2).reshape(
        lead + (2 * SSD_HEADS,))
    w_gate = _pad_lanes(jnp.concatenate([by_group(dt, (depth, D_MODEL)), ig, fg], axis=-1),
                        LANES).astype(BF16)
    hp = jnp.stack([_pad_lanes(by_group(ssd_dt_bias, (depth,)), LANES),
                    _pad_lanes(by_group(ssd_a_log, (depth,)), LANES)], axis=1)
    hp = jnp.pad(hp, ((0, 0), (0, 6), (0, 0)))
    dsk = jnp.repeat(ssd_d, SSD_HEAD_DIM, axis=-1).reshape(depth, 1, SSD_WIDTH)
    gbias = _pad_lanes(jnp.concatenate([jnp.zeros((depth, ML_GATE_LANE0), F32),
                                        mlstm_i_bias.reshape(depth, 2 * ML_HEADS),
                                        mlstm_f_bias.reshape(depth, 2 * ML_HEADS)], axis=-1),
                       LANES).reshape(depth, 1, LANES)
    return w_main, w_gate, hp, dsk, gbias


def _conv_shift_matrices():
    taps = [k for k in range(SSD_CONV) if k != SSD_CONV // 2]
    sh = np.zeros((len(taps) * CONV_ROWS, CONV_WIN), np.float32)
    for i, k in enumerate(taps):
        sh[i * CONV_ROWS + np.arange(CONV_ROWS), np.arange(CONV_ROWS) + k - SSD_CONV // 2 + CONV_PAD] = 1.0
    return jnp.asarray(sh, BF16)


def kernel(x, norm1_w, w_in, ssd_conv_w, ssd_conv_b, ssd_dt_bias, ssd_a_log, ssd_d, ssd_norm_w,
           mlstm_i_bias, mlstm_f_bias, mlstm_norm_w, w_out, norm2_w, w_up, ffn_conv_w, ffn_conv_b,
           w_down, norm_f_w):
    bsz, seq, d = x.shape
    depth = w_in.shape[0]
    m = bsz * seq
    assert d == D_MODEL and seq % CHUNK == 0
    tm = min(m, 1024)
    t_ffn = min(seq, 512)

    w_main, w_gate, hp, dsk, gbias = _prep_weights(w_in, ssd_dt_bias, ssd_a_log, ssd_d,
                                                   mlstm_i_bias, mlstm_f_bias)
    w_out_b = w_out.astype(BF16)
    w_up_b = w_up.astype(BF16)
    w_down_b = w_down.astype(BF16)

    row3 = lambda p: p.reshape(depth, 1, -1)
    ssd_cb, ssd_nw, ml_nw, n2w, ffn_cb = (row3(p) for p in (ssd_conv_b, ssd_norm_w, mlstm_norm_w,
                                                            norm2_w, ffn_conv_b))

    shifts = _conv_shift_matrices()

    xf = x.reshape(m, d)
    h = _rmsnorm(xf, norm1_w[0], tm)
    for l in range(depth):
        proj, gates = _inproj(h, w_main, w_gate, l, tm, 1024)
        proj3 = proj.reshape(bsz, seq, N_MAIN)
        gates3 = gates.reshape(bsz, seq, N_GATE)
        y_ssd = _ssd_mixer(proj3, gates3, ssd_conv_w, ssd_cb, hp, dsk, ssd_nw, shifts, l)
        y_ml = _mlstm_mixer(proj3, gates3, gbias, ml_nw, l)
        x1, h2 = _outproj(xf, y_ssd.reshape(m, SSD_WIDTH), y_ml.reshape(m, ML_WIDTH), w_out_b, n2w, l,
                          min(m, 512))
        last = l == depth - 1
        nw = norm_f_w if last else norm1_w[l + 1]
        xf, h = _ffn(x1, h2, w_up_b, ffn_conv_w, ffn_cb, w_down_b, nw.reshape(1, d), l, seq, t_ffn,
                     512, F32 if last else BF16)
    return h.reshape(bsz, seq, d)
```

```python
import functools

import jax
import jax.numpy as jnp
import numpy as np
from jax import lax
from jax.experimental import pallas as pl
from jax.experimental.pallas import tpu as pltpu

EPS = 1e-6
NEG = -1e30
LOG2E = 1.4426950408889634
F32 = jnp.float32
BF16 = jnp.bfloat16

D_MODEL = 2048
SSD_WIDTH = 2048
SSD_HEAD_DIM = 64
SSD_HEADS = 32
SSD_GROUPS = 4
SSD_STATE = 128
SSD_HEADS_PER_GROUP = SSD_HEADS // SSD_GROUPS
SSD_GROUP_WIDTH = SSD_WIDTH // SSD_GROUPS
SSD_CONV = 5
ML_HEADS = 8
ML_DK = 128
ML_DV = 256
ML_WIDTH = ML_HEADS * ML_DV
ML_QK_SCALE = ML_DK ** -0.5
ML_HEADS_PER_STEP = 2
D_FF = 5632
FFN_CONV = 3
CHUNK = 128
LANES = 128
BF16_ROWS = 16
CONV_ROWS = 64
CONV_PAD = 32
CONV_WIN = CONV_ROWS + 2 * CONV_PAD
N_MAIN = 2 * SSD_WIDTH + 2 * SSD_GROUPS * SSD_STATE + 2 * ML_HEADS * ML_DK + 2 * ML_WIDTH
N_GATE = LANES
ML_GATE_LANE0 = 2 * SSD_HEADS
VMEM_LIMIT = 56 * 1024 * 1024


def _params(*sem):
    return pltpu.CompilerParams(dimension_semantics=sem, vmem_limit_bytes=VMEM_LIMIT)


def _sigmoid(x):
    return 0.5 + 0.5 * jnp.tanh(0.5 * x)


def _silu(x):
    hx = 0.5 * x
    return hx + hx * jnp.tanh(hx)


def _softplus(x):
    return jnp.maximum(x, 0.0) + jnp.log(1.0 + jnp.exp(-jnp.abs(x)))


def _dot(a, b):
    return jnp.dot(a, b, preferred_element_type=F32)


def _split3(x):
    hi = x.astype(BF16)
    r = x - hi.astype(F32)
    mid = r.astype(BF16)
    lo = (r - mid.astype(F32)).astype(BF16)
    return hi, mid, lo


def _cumdot(tri3, x):
    return _dot(tri3, jnp.concatenate(_split3(x), axis=0))


def _tri3(fwd):
    row = lax.broadcasted_iota(jnp.int32, (CHUNK, 3 * CHUNK), 0)
    col = lax.broadcasted_iota(jnp.int32, (CHUNK, 3 * CHUNK), 1) % CHUNK
    keep = (col <= row) if fwd else (col >= row)
    return jnp.where(keep, 1.0, 0.0).astype(BF16)


def _lane_col(x, lane_iota, lane):
    return jnp.sum(jnp.where(lane_iota == lane, x, 0.0), axis=1, keepdims=True)


def _rmsnorm_kernel(x_ref, w_ref, o_ref):
    x = x_ref[...]
    ms = jnp.mean(x * x, axis=1, keepdims=True)
    o_ref[...] = (x * lax.rsqrt(ms + EPS) * w_ref[...]).astype(o_ref.dtype)


def _rmsnorm(x, w, tm):
    m, d = x.shape
    return pl.pallas_call(
        _rmsnorm_kernel,
        grid=(m // tm,),
        in_specs=[pl.BlockSpec((tm, d), lambda i: (i, 0)), pl.BlockSpec((1, d), lambda i: (0, 0))],
        out_specs=pl.BlockSpec((tm, d), lambda i: (i, 0)),
        out_shape=jax.ShapeDtypeStruct((m, d), BF16),
        compiler_params=_params("parallel"),
        name="rmsnorm_in",
    )(x, w.reshape(1, d))


def _inproj_kernel(h_ref, w_ref, wg_ref, p_ref, g_ref):
    h = h_ref[...]
    p_ref[...] = _dot(h, w_ref[...]).astype(p_ref.dtype)

    @pl.when(pl.program_id(1) == 0)
    def _():
        g_ref[...] = _dot(h, wg_ref[...])


def _inproj(h, w_main, w_gate, layer, tm, tn):
    m, d = h.shape
    return pl.pallas_call(
        _inproj_kernel,
        grid=(m // tm, N_MAIN // tn),
        in_specs=[pl.BlockSpec((tm, d), lambda i, j: (i, 0)),
                  pl.BlockSpec((None, d, tn), lambda i, j: (layer, 0, j)),
                  pl.BlockSpec((None, d, N_GATE), lambda i, j: (layer, 0, 0))],
        out_specs=[pl.BlockSpec((tm, tn), lambda i, j: (i, j)),
                   pl.BlockSpec((tm, N_GATE), lambda i, j: (i, 0))],
        out_shape=[jax.ShapeDtypeStruct((m, N_MAIN), BF16),
                   jax.ShapeDtypeStruct((m, N_GATE), F32)],
        compiler_params=_params("parallel", "arbitrary"),
        name="in_proj",
    )(h, w_main, w_gate)


def _ssd_chunk(c, fwd, refs, consts):
    (xb_ref, bt_ref, cc_ref, acs_ref, rt_ref, sel_ref, y_ref, s_ref) = refs
    (mask, lane0, left) = consts
    rows = pl.ds(pl.multiple_of(c * CHUNK, CHUNK), CHUNK)
    acs = acs_ref[rows, :]
    cc = cc_ref[rows, :]
    b_t = bt_ref[c]
    scores = _dot(cc, b_t.astype(BF16))
    s_prev = s_ref[...]
    y_off = _dot(cc, s_prev.astype(BF16))
    base = 0 if fwd else SSD_HEADS_PER_GROUP
    edge = CHUNK - 1 if fwd else 0
    for q in range(SSD_HEADS_PER_GROUP // 2):
        cols = slice(q * LANES, (q + 1) * LANES)
        xb = xb_ref[rows, cols]
        zero = jnp.zeros_like(xb)
        xbd = jnp.concatenate([jnp.where(left, xb, zero), jnp.where(left, zero, xb)], axis=0)
        col_a, last, ms, bws = [], [], [], []
        for e in range(2):
            hd = base + 2 * q + e
            ca = jnp.sum(acs * sel_ref[hd], axis=1, keepdims=True)
            r_row = rt_ref[c, pl.ds(lane0 + hd, 1), :]
            la = ca[edge:edge + 1, :]
            ms.append((scores * jnp.exp2(jnp.where(mask, ca + r_row, NEG))).astype(BF16))
            bws.append((b_t * jnp.exp2(r_row + la)).astype(BF16))
            col_a.append(ca)
            last.append(la)
        lhs = jnp.concatenate([jnp.concatenate(ms, axis=1), jnp.concatenate(bws, axis=1)], axis=0)
        res = _dot(lhs, xbd)
        a_sel = jnp.where(left, col_a[0], col_a[1])
        y_ref[rows, cols] = res[:CHUNK] + jnp.exp2(a_sel) * y_off[:, cols]
        last_sel = jnp.where(left[0:1, :], last[0], last[1])
        s_ref[:, cols] = jnp.exp2(last_sel) * s_prev[:, cols] + res[CHUNK:]


def _ssd_kernel(xs_ref, z_ref, b_ref, c_ref, gt_ref, cwx_ref, cwb_ref, cwc_ref, cbx_ref, cbb_ref,
                cbc_ref, hp_ref, dsk_ref, nw_ref, sh_ref, o_ref,
                stage_ref, xc_ref, xb_ref, bt_ref, cc_ref, acs_ref, rt_ref, yf_ref, yb_ref,
                sf_ref, sb_ref, tri_ref, sel_ref):
    seq = xs_ref.shape[0]
    nc = seq // CHUNK
    rb = min(seq, 256)
    gw = SSD_GROUP_WIDTH
    n = SSD_STATE
    width = gw + 2 * n

    stage_ref[0:CONV_PAD, :] = jnp.zeros((CONV_PAD, width), BF16)
    stage_ref[CONV_PAD + seq:2 * CONV_PAD + seq, :] = jnp.zeros((CONV_PAD, width), BF16)
    for r in range(0, seq, rb):
        dst = slice(CONV_PAD + r, CONV_PAD + r + rb)
        stage_ref[dst, 0:gw] = xs_ref[r:r + rb, :]
        stage_ref[dst, gw:gw + n] = b_ref[r:r + rb, :]
        stage_ref[dst, gw + n:width] = c_ref[r:r + rb, :]

    taps = [k for k in range(SSD_CONV) if k != SSD_CONV // 2]

    cb = CONV_ROWS

    def conv(r, c0, c1, weights, bias):
        win = stage_ref[r:r + CONV_WIN, c0:c1]
        shifted = _dot(sh_ref[...], win)
        centre = stage_ref[CONV_PAD + r:CONV_PAD + r + cb, c0:c1].astype(F32)
        acc = bias + weights[SSD_CONV // 2] * centre
        for i, k in enumerate(taps):
            acc = acc + weights[k] * shifted[i * cb:(i + 1) * cb]
        return _silu(acc)

    half = 2 * LANES
    wbc = [jnp.concatenate([cwb_ref[k:k + 1, :], cwc_ref[k:k + 1, :]], axis=1) for k in range(SSD_CONV)]
    bbc = jnp.concatenate([cbb_ref[...], cbc_ref[...]], axis=1)
    for c in range(nc):
        bcs = []
        for r in range(c * CHUNK, (c + 1) * CHUNK, cb):
            for c0 in range(0, gw, half):
                wx = [cwx_ref[k:k + 1, c0:c0 + half] for k in range(SSD_CONV)]
                xc = conv(r, c0, c0 + half, wx, cbx_ref[:, c0:c0 + half])
                xc_ref[r:r + cb, c0:c0 + half] = xc
                xb_ref[r:r + cb, c0:c0 + half] = xc.astype(BF16)
            bc = conv(r, gw, width, wbc, bbc)
            cc_ref[r:r + cb, :] = bc[:, n:].astype(BF16)
            bcs.append(bc[:, :n])
        bt_ref[c] = jnp.concatenate(bcs, axis=0).T

    row = lax.broadcasted_iota(jnp.int32, (CHUNK, CHUNK), 0)
    col = lax.broadcasted_iota(jnp.int32, (CHUNK, CHUNK), 1)
    left = col < SSD_HEAD_DIM

    hg = SSD_HEADS_PER_GROUP
    lane0 = pl.program_id(1) * (2 * hg)
    fwd_lane = (col % (2 * hg)) < hg
    for hd in range(2 * hg):
        sel_ref[hd] = jnp.where(col == lane0 + hd, 1.0, 0.0)
    tri_ref[0] = _tri3(True)
    tri_ref[1] = _tri3(False)

    def cumsum_body(c, carry):
        rows = pl.ds(pl.multiple_of(c * CHUNK, CHUNK), CHUNK)
        dt = _softplus(gt_ref[rows, :] + hp_ref[0:1, :])
        a_dt = dt * (-jnp.exp(hp_ref[1:2, :]))
        acs = jnp.where(fwd_lane, _cumdot(tri_ref[0], a_dt), _cumdot(tri_ref[1], a_dt))
        acs_ref[rows, :] = acs * LOG2E
        rt_ref[c] = ((jnp.log(dt) - acs) * LOG2E).T
        return carry

    lax.fori_loop(0, nc, cumsum_body, 0, unroll=4)

    sf_ref[...] = jnp.zeros_like(sf_ref)
    sb_ref[...] = jnp.zeros_like(sb_ref)

    consts_f = (col <= row, lane0, left)
    consts_b = (col >= row, lane0, left)
    refs_f = (xb_ref, bt_ref, cc_ref, acs_ref, rt_ref, sel_ref, yf_ref, sf_ref)
    refs_b = (xb_ref, bt_ref, cc_ref, acs_ref, rt_ref, sel_ref, yb_ref, sb_ref)

    def body(j, carry):
        _ssd_chunk(j, True, refs_f, consts_f)
        _ssd_chunk(nc - 1 - j, False, refs_b, consts_b)
        return carry

    lax.fori_loop(0, nc, body, 0, unroll=2)

    for r in range(0, seq, rb):
        rows = slice(r, r + rb)
        y = yf_ref[rows, :] + yb_ref[rows, :] + xc_ref[rows, :] * dsk_ref[...]
        y = y * _silu(z_ref[rows, :].astype(F32))
        ms = jnp.mean(y * y, axis=1, keepdims=True)
        o_ref[rows, :] = (y * lax.rsqrt(ms + EPS) * nw_ref[...]).astype(o_ref.dtype)


def _ssd_mixer(proj, gates, cw, cb, hp, dsk, nw, shifts, layer):
    bsz, seq, _ = proj.shape
    gw, n, g = SSD_GROUP_WIDTH, SSD_STATE, SSD_GROUPS
    xs0, z0 = 0, SSD_WIDTH // gw
    b0 = 2 * SSD_WIDTH // n
    c0 = b0 + g
    cwb0 = SSD_WIDTH // n
    in_specs = [
        pl.BlockSpec((None, seq, gw), lambda b, i: (b, 0, xs0 + i)),
        pl.BlockSpec((None, seq, gw), lambda b, i: (b, 0, z0 + i)),
        pl.BlockSpec((None, seq, n), lambda b, i: (b, 0, b0 + i)),
        pl.BlockSpec((None, seq, n), lambda b, i: (b, 0, c0 + i)),
        pl.BlockSpec((None, seq, LANES), lambda b, i: (b, 0, 0)),
        pl.BlockSpec((None, SSD_CONV, gw), lambda b, i: (layer, 0, i)),
        pl.BlockSpec((None, SSD_CONV, n), lambda b, i: (layer, 0, cwb0 + i)),
        pl.BlockSpec((None, SSD_CONV, n), lambda b, i: (layer, 0, cwb0 + g + i)),
        pl.BlockSpec((None, 1, gw), lambda b, i: (layer, 0, i)),
        pl.BlockSpec((None, 1, n), lambda b, i: (layer, 0, cwb0 + i)),
        pl.BlockSpec((None, 1, n), lambda b, i: (layer, 0, cwb0 + g + i)),
        pl.BlockSpec((None, 8, LANES), lambda b, i: (layer, 0, 0)),
        pl.BlockSpec((None, 1, gw), lambda b, i: (layer, 0, i)),
        pl.BlockSpec((None, 1, gw), lambda b, i: (layer, 0, i)),
        pl.BlockSpec(shifts.shape, lambda b, i: (0, 0)),
    ]
    scratch = [
        pltpu.VMEM((seq + 2 * CONV_PAD, gw + 2 * n), BF16),
        pltpu.VMEM((seq, gw), F32),
        pltpu.VMEM((seq, gw), BF16),
        pltpu.VMEM((seq // CHUNK, n, CHUNK), F32),
        pltpu.VMEM((seq, n), BF16),
        pltpu.VMEM((seq, LANES), F32),
        pltpu.VMEM((seq // CHUNK, LANES, CHUNK), F32),
        pltpu.VMEM((seq, gw), F32),
        pltpu.VMEM((seq, gw), F32),
        pltpu.VMEM((n, gw), F32),
        pltpu.VMEM((n, gw), F32),
        pltpu.VMEM((2, CHUNK, 3 * CHUNK), BF16),
        pltpu.VMEM((2 * SSD_HEADS_PER_GROUP, CHUNK, LANES), F32),
    ]
    return pl.pallas_call(
        _ssd_kernel,
        grid=(bsz, g),
        in_specs=in_specs,
        out_specs=pl.BlockSpec((None, seq, gw), lambda b, i: (b, 0, i)),
        out_shape=jax.ShapeDtypeStruct((bsz, seq, SSD_WIDTH), BF16),
        scratch_shapes=scratch,
        compiler_params=_params("parallel", "parallel"),
        name="ssd_mixer",
    )(proj, proj, proj, proj, gates, cw, cw, cw, cb, cb, cb, hp, dsk, nw, shifts)


def _mlstm_gates(c, fwd, head, hh, m_state, refs, consts):
    (q_ref, k_t_ref, gi_t_ref, bs_ref, bs_t_ref, sb_ref, wi_ref, en_ref, kw_ref, dec_ref) = refs
    (mask, lane_iota) = consts
    nc = dec_ref.shape[1] // 2
    rows = pl.ds(pl.multiple_of(c * CHUNK, CHUNK), CHUNK)
    d = 0 if fwd else 1
    edge = CHUNK - 1 if fwd else 0
    f_lane = ML_GATE_LANE0 + (2 + d) * ML_HEADS + head
    li_row = gi_t_ref[c, pl.ds(ML_GATE_LANE0 + d * ML_HEADS + head, 1), :]
    b_row = bs_t_ref[d, c, pl.ds(f_lane, 1), :]
    d_row = li_row - b_row
    b_col = _lane_col(bs_ref[d, rows, :], lane_iota, f_lane)
    dm = jnp.where(mask, d_row, NEG)
    g = jnp.maximum(jnp.max(dm, axis=1, keepdims=True), m_state)
    k_t = k_t_ref[hh, c]
    q = q_ref[rows, hh * ML_DK:(hh + 1) * ML_DK]
    s = _dot(q, k_t) * (jnp.exp(dm - g) * ML_QK_SCALE)
    sb_ref[hh, d, rows, :] = s.astype(BF16)
    wi_ref[hh, d, rows, :] = jnp.broadcast_to(jnp.exp(m_state - g) * ML_QK_SCALE, (CHUNK, LANES))
    en_ref[hh, d, rows, :] = jnp.broadcast_to(jnp.exp(-(b_col + g)), (CHUNK, LANES))
    g_last = g[edge:edge + 1, :]
    kw_ref[hh, d, c] = (k_t.astype(F32) * jnp.exp(d_row - g_last)).astype(BF16)
    dec_ref[hh, pl.ds(d * nc + c, 1), :] = jnp.broadcast_to(jnp.exp(m_state - g_last),
                                                            (1, dec_ref.shape[2]))
    return b_col[edge:edge + 1, :] + g_last


def _mlstm_chunk(c, fwd, hh, ones, refs):
    (q_ref, v_ref, sb_ref, wi_ref, en_ref, kw_ref, dec_ref, h_ref, cs_ref) = refs
    nc = dec_ref.shape[1] // 2
    rows = pl.ds(pl.multiple_of(c * CHUNK, CHUNK), CHUNK)
    d = 0 if fwd else 1
    vcols = slice(hh * ML_DV, (hh + 1) * ML_DV)
    v_aug = jnp.concatenate([v_ref[rows, vcols], ones], axis=1)
    c_aug = cs_ref[hh]
    wi = wi_ref[hh, d, rows, :]
    both = _dot(jnp.concatenate([sb_ref[hh, d, rows, :], kw_ref[hh, d, c]], axis=0), v_aug)
    q = q_ref[rows, hh * ML_DK:(hh + 1) * ML_DK]
    na = both[:CHUNK] + jnp.concatenate([wi, wi, wi], axis=1) * _dot(q, c_aug.astype(BF16))
    r = 1.0 / jnp.maximum(jnp.abs(na[:, ML_DV:]), en_ref[hh, d, rows, :])
    h_ref[rows, vcols] = na[:, :ML_DV] * jnp.concatenate([r, r], axis=1)
    cs_ref[hh] = dec_ref[hh, pl.ds(d * nc + c, 1), :] * c_aug + both[CHUNK:]


def _mlstm_kernel(q_ref, k_ref, v_ref, o_ref, gt_ref, gb_ref, nw_ref, y_ref,
                  gi_t_ref, bs_ref, bs_t_ref, k_t_ref, sb_ref, wi_ref, en_ref, kw_ref, dec_ref,
                  hf_ref, hb_ref, cf_ref, cb_ref, tri_ref):
    seq = q_ref.shape[0]
    nc = seq // CHUNK
    rb = min(seq, 256)
    hps = ML_HEADS_PER_STEP
    head0 = pl.program_id(1) * hps

    @pl.when(head0 == 0)
    def _():
        tri_ref[0] = _tri3(True)
        tri_ref[1] = _tri3(False)

        def cumsum_body(c, carry):
            rows = pl.ds(pl.multiple_of(c * CHUNK, CHUNK), CHUNK)
            g = gt_ref[rows, :] + gb_ref[...]
            gi_t_ref[c] = g.T
            lf = -_softplus(-g)
            b_f = _cumdot(tri_ref[0], lf)
            b_b = _cumdot(tri_ref[1], lf)
            bs_ref[0, rows, :] = b_f
            bs_ref[1, rows, :] = b_b
            bs_t_ref[0, c] = b_f.T
            bs_t_ref[1, c] = b_b.T
            return carry

        lax.fori_loop(0, nc, cumsum_body, 0, unroll=2)

    def transpose_body(c, carry):
        rows = pl.ds(pl.multiple_of(c * CHUNK, CHUNK), CHUNK)
        for hh in range(hps):
            k_t_ref[hh, c] = k_ref[rows, hh * ML_DK:(hh + 1) * ML_DK].astype(F32).T.astype(BF16)
        return carry

    lax.fori_loop(0, nc, transpose_body, 0, unroll=2)
    cf_ref[...] = jnp.zeros_like(cf_ref)
    cb_ref[...] = jnp.zeros_like(cb_ref)

    row = lax.broadcasted_iota(jnp.int32, (CHUNK, CHUNK), 0)
    col = lax.broadcasted_iota(jnp.int32, (CHUNK, CHUNK), 1)
    gate_refs = (q_ref, k_t_ref, gi_t_ref, bs_ref, bs_t_ref, sb_ref, wi_ref, en_ref, kw_ref, dec_ref)

    def gate_body(j, carry):
        out = []
        for hh in range(hps):
            out.append(_mlstm_gates(j, True, head0 + hh, hh, carry[2 * hh], gate_refs,
                                    (col <= row, col)))
            out.append(_mlstm_gates(nc - 1 - j, False, head0 + hh, hh, carry[2 * hh + 1], gate_refs,
                                    (col >= row, col)))
        return tuple(out)

    m_init = jnp.full((1, 1), NEG, F32)
    lax.fori_loop(0, nc, gate_body, (m_init,) * (2 * hps), unroll=2)

    ones = jnp.ones((CHUNK, LANES), BF16)
    shared = (q_ref, v_ref, sb_ref, wi_ref, en_ref, kw_ref, dec_ref)

    def body(j, carry):
        for hh in range(hps):
            _mlstm_chunk(j, True, hh, ones, shared + (hf_ref, cf_ref))
            _mlstm_chunk(nc - 1 - j, False, hh, ones, shared + (hb_ref, cb_ref))
        return carry

    lax.fori_loop(0, nc, body, 0)

    for r in range(0, seq, rb):
        rows = slice(r, r + rb)
        for hh in range(hps):
            vcols = slice(hh * ML_DV, (hh + 1) * ML_DV)
            hs = hf_ref[rows, vcols] + hb_ref[rows, vcols]
            ms = jnp.mean(hs * hs, axis=1, keepdims=True)
            hn = hs * lax.rsqrt(ms + EPS) * nw_ref[:, vcols]
            gate = _sigmoid(o_ref[rows, vcols].astype(F32))
            y_ref[rows, vcols] = (gate * hn).astype(y_ref.dtype)


def _mlstm_mixer(proj, gates, gbias, nw, layer):
    bsz, seq, _ = proj.shape
    hps = ML_HEADS_PER_STEP
    qk_w, v_w = hps * ML_DK, hps * ML_DV
    q_off = 2 * SSD_WIDTH + 2 * SSD_GROUPS * SSD_STATE
    q0 = q_off // qk_w
    k0 = q0 + ML_HEADS // hps
    v0 = (q_off + 2 * ML_HEADS * ML_DK) // v_w
    o0 = v0 + ML_HEADS // hps
    in_specs = [
        pl.BlockSpec((None, seq, qk_w), lambda b, h: (b, 0, q0 + h)),
        pl.BlockSpec((None, seq, qk_w), lambda b, h: (b, 0, k0 + h)),
        pl.BlockSpec((None, seq, v_w), lambda b, h: (b, 0, v0 + h)),
        pl.BlockSpec((None, seq, v_w), lambda b, h: (b, 0, o0 + h)),
        pl.BlockSpec((None, seq, LANES), lambda b, h: (b, 0, 0)),
        pl.BlockSpec((None, 1, LANES), lambda b, h: (layer, 0, 0)),
        pl.BlockSpec((None, 1, v_w), lambda b, h: (layer, 0, h)),
    ]
    nc = seq // CHUNK
    aug = ML_DV + LANES
    scratch = [
        pltpu.VMEM((nc, LANES, CHUNK), F32),
        pltpu.VMEM((2, seq, LANES), F32),
        pltpu.VMEM((2, nc, LANES, CHUNK), F32),
        pltpu.VMEM((hps, nc, ML_DK, CHUNK), BF16),
        pltpu.VMEM((hps, 2, seq, CHUNK), BF16),
        pltpu.VMEM((hps, 2, seq, LANES), F32),
        pltpu.VMEM((hps, 2, seq, LANES), F32),
        pltpu.VMEM((hps, 2, nc, ML_DK, CHUNK), BF16),
        pltpu.VMEM((hps, 2 * nc, aug), F32),
        pltpu.VMEM((seq, v_w), F32),
        pltpu.VMEM((seq, v_w), F32),
        pltpu.VMEM((hps, ML_DK, aug), F32),
        pltpu.VMEM((hps, ML_DK, aug), F32),
        pltpu.VMEM((2, CHUNK, 3 * CHUNK), BF16),
    ]
    return pl.pallas_call(
        _mlstm_kernel,
        grid=(bsz, ML_HEADS // hps),
        in_specs=in_specs,
        out_specs=pl.BlockSpec((None, seq, v_w), lambda b, h: (b, 0, h)),
        out_shape=jax.ShapeDtypeStruct((bsz, seq, ML_WIDTH), BF16),
        scratch_shapes=scratch,
        compiler_params=_params("parallel", "arbitrary"),
        name="mlstm_mixer",
    )(proj, proj, proj, proj, gates, gbias, nw)


def _outproj_kernel(x_ref, ys_ref, ym_ref, ws_ref, wm_ref, nw_ref, x1_ref, h_ref):
    x1 = x_ref[...] + _dot(ys_ref[...], ws_ref[...]) + _dot(ym_ref[...], wm_ref[...])
    x1_ref[...] = x1
    ms = jnp.mean(x1 * x1, axis=1, keepdims=True)
    h_ref[...] = (x1 * lax.rsqrt(ms + EPS) * nw_ref[...]).astype(h_ref.dtype)


def _outproj(x, y_ssd, y_ml, w_out, nw, layer, tm):
    m, d = x.shape
    return pl.pallas_call(
        _outproj_kernel,
        grid=(m // tm,),
        in_specs=[pl.BlockSpec((tm, d), lambda i: (i, 0)),
                  pl.BlockSpec((tm, SSD_WIDTH), lambda i: (i, 0)),
                  pl.BlockSpec((tm, ML_WIDTH), lambda i: (i, 0)),
                  pl.BlockSpec((None, SSD_WIDTH, d), lambda i: (layer, 0, 0),
                               pipeline_mode=pl.Buffered(1)),
                  pl.BlockSpec((None, ML_WIDTH, d), lambda i: (layer, 1, 0),
                               pipeline_mode=pl.Buffered(1)),
                  pl.BlockSpec((None, 1, d), lambda i: (layer, 0, 0))],
        out_specs=[pl.BlockSpec((tm, d), lambda i: (i, 0)),
                   pl.BlockSpec((tm, d), lambda i: (i, 0))],
        out_shape=[jax.ShapeDtypeStruct((m, d), F32), jax.ShapeDtypeStruct((m, d), BF16)],
        compiler_params=_params("parallel"),
        name="out_proj",
    )(x, y_ssd, y_ml, w_out, w_out, nw)


def _ffn_kernel(tiles_per_seq, h_ref, hp_ref, hn_ref, x_ref, wg_ref, wv_ref, cwg_ref, cwv_ref,
                cbg_ref, cbv_ref, wd_ref, nw_ref, x2_ref, o_ref, hext_ref, ug_ref, uv_ref, acc_ref):
    i = pl.program_id(0)
    j = pl.program_id(1)
    t = h_ref.shape[0]
    halo = BF16_ROWS

    @pl.when(j == 0)
    def _():
        pos = i % tiles_per_seq
        prev = hp_ref[...]
        nxt = hn_ref[...]
        prev = jnp.where(pos == 0, jnp.zeros_like(prev), prev)
        nxt = jnp.where(pos == tiles_per_seq - 1, jnp.zeros_like(nxt), nxt)
        sub = lax.broadcasted_iota(jnp.int32, prev.shape, 0)
        hext_ref[0:t, :] = h_ref[...]
        hext_ref[t:t + halo, :] = jnp.where(sub < halo // 2, nxt, prev)
        acc_ref[...] = jnp.zeros_like(acc_ref)

    hext = hext_ref[...]
    for u_ref, w_ref in ((ug_ref, wg_ref), (uv_ref, wv_ref)):
        u = _dot(hext, w_ref[...])
        u_ref[halo:2 * halo + t, :] = u
        u_ref[halo // 2:halo, :] = u[t + halo // 2:t + halo, :]

    def conv(u_ref, w_ref, b_ref):
        acc = b_ref[...] + w_ref[0:1, :] * u_ref[halo - 1:halo - 1 + t, :]
        acc = acc + w_ref[1:2, :] * u_ref[halo:halo + t, :]
        return acc + w_ref[2:3, :] * u_ref[halo + 1:halo + 1 + t, :]

    act = _silu(conv(ug_ref, cwg_ref, cbg_ref)) * conv(uv_ref, cwv_ref, cbv_ref)
    acc_ref[...] += _dot(act.astype(BF16), wd_ref[...])

    @pl.when(j == pl.num_programs(1) - 1)
    def _():
        x2 = x_ref[...] + acc_ref[...]
        x2_ref[...] = x2
        ms = jnp.mean(x2 * x2, axis=1, keepdims=True)
        o_ref[...] = (x2 * lax.rsqrt(ms + EPS) * nw_ref[...]).astype(o_ref.dtype)


def _ffn(x1, h2, w_up, cw, cb, w_down, nw, layer, seq, t, f, out_dtype):
    m, d = x1.shape
    nj = D_FF // f
    tiles_per_seq = seq // t
    hb = t // BF16_ROWS
    last_blk = m // BF16_ROWS - 1
    in_specs = [
        pl.BlockSpec((t, d), lambda i, j: (i, 0)),
        pl.BlockSpec((BF16_ROWS, d), lambda i, j: (jnp.maximum(i * hb - 1, 0), 0)),
        pl.BlockSpec((BF16_ROWS, d), lambda i, j: (jnp.minimum((i + 1) * hb, last_blk), 0)),
        pl.BlockSpec((t, d), lambda i, j: (i, 0)),
        pl.BlockSpec((None, d, f), lambda i, j: (layer, 0, j)),
        pl.BlockSpec((None, d, f), lambda i, j: (layer, 0, nj + j)),
        pl.BlockSpec((None, FFN_CONV, f), lambda i, j: (layer, 0, j)),
        pl.BlockSpec((None, FFN_CONV, f), lambda i, j: (layer, 0, nj + j)),
        pl.BlockSpec((None, 1, f), lambda i, j: (layer, 0, j)),
        pl.BlockSpec((None, 1, f), lambda i, j: (layer, 0, nj + j)),
        pl.BlockSpec((None, f, d), lambda i, j: (layer, j, 0)),
        pl.BlockSpec((1, d), lambda i, j: (0, 0)),
    ]
    scratch = [
        pltpu.VMEM((t + BF16_ROWS, d), BF16),
        pltpu.VMEM((t + 2 * BF16_ROWS, f), F32),
        pltpu.VMEM((t + 2 * BF16_ROWS, f), F32),
        pltpu.VMEM((t, d), F32),
    ]
    return pl.pallas_call(
        functools.partial(_ffn_kernel, tiles_per_seq),
        grid=(m // t, nj),
        in_specs=in_specs,
        out_specs=[pl.BlockSpec((t, d), lambda i, j: (i, 0)),
                   pl.BlockSpec((t, d), lambda i, j: (i, 0))],
        out_shape=[jax.ShapeDtypeStruct((m, d), F32), jax.ShapeDtypeStruct((m, d), out_dtype)],
        scratch_shapes=scratch,
        compiler_params=_params("parallel", "arbitrary"),
        name="conv_ffn",
    )(h2, h2, h2, x1, w_up, w_up, cw, cw, cb, cb, w_down, nw)


def _pad_lanes(a, width):
    return jnp.pad(a, [(0, 0)] * (a.ndim - 1) + [(0, width - a.shape[-1])])


def _prep_weights(w_in, ssd_dt_bias, ssd_a_log, ssd_d, mlstm_i_bias, mlstm_f_bias):
    depth = w_in.shape[0]
    sizes = (SSD_WIDTH, SSD_WIDTH, SSD_GROUPS * SSD_STATE, SSD_GROUPS * SSD_STATE, 2 * SSD_HEADS,
             ML_HEADS * ML_DK, ML_HEADS * ML_DK, ML_WIDTH, ML_WIDTH, 2 * ML_HEADS, 2 * ML_HEADS)
    offs = np.cumsum((0,) + sizes)
    part = lambda n: w_in[:, :, offs[n]:offs[n + 1]]
    dt, ig, fg = part(4), part(9), part(10)
    n_a = 2 * SSD_WIDTH + 2 * SSD_GROUPS * SSD_STATE
    w_main = jnp.concatenate([w_in[:, :, :n_a].astype(BF16),
                              w_in[:, :, offs[5]:offs[9]].astype(BF16)],
                             axis=-1)
    hg = SSD_HEADS_PER_GROUP
    by_group = lambda p, lead: jnp.swapaxes(p.reshape(lead + (2, SSD_GROUPS, hg)), -3, -2).reshape(
        lead + (2 * SSD_HEADS,))
    w_gate = _pad_lanes(jnp.concatenate([by_group(dt, (depth, D_MODEL)), ig, fg], axis=-1),
                        LANES).astype(BF16)
    hp = jnp.stack([_pad_lanes(by_group(ssd_dt_bias, (depth,)), LANES),
                    _pad_lanes(by_group(ssd_a_log, (depth,)), LANES)], axis=1)
    hp = jnp.pad(hp, ((0, 0), (0, 6), (0, 0)))
    dsk = jnp.repeat(ssd_d, SSD_HEAD_DIM, axis=-1).reshape(depth, 1, SSD_WIDTH)
    gbias = _pad_lanes(jnp.concatenate([jnp.zeros((depth, ML_GATE_LANE0), F32),
                                        mlstm_i_bias.reshape(depth, 2 * ML_HEADS),
                                        mlstm_f_bias.reshape(depth, 2 * ML_HEADS)], axis=-1),
                       LANES).reshape(depth, 1, LANES)
    return w_main, w_gate, hp, dsk, gbias


def _conv_shift_matrices():
    taps = [k for k in range(SSD_CONV) if k != SSD_CONV // 2]
    sh = np.zeros((len(taps) * CONV_ROWS, CONV_WIN), np.float32)
    for i, k in enumerate(taps):
        sh[i * CONV_ROWS + np.arange(CONV_ROWS), np.arange(CONV_ROWS) + k - SSD_CONV // 2 + CONV_PAD] = 1.0
    return jnp.asarray(sh, BF16)


def kernel(x, norm1_w, w_in, ssd_conv_w, ssd_conv_b, ssd_dt_bias, ssd_a_log, ssd_d, ssd_norm_w,
           mlstm_i_bias, mlstm_f_bias, mlstm_norm_w, w_out, norm2_w, w_up, ffn_conv_w, ffn_conv_b,
           w_down, norm_f_w):
    bsz, seq, d = x.shape
    depth = w_in.shape[0]
    m = bsz * seq
    assert d == D_MODEL and seq % CHUNK == 0
    tm = min(m, 1024)
    t_ffn = min(seq, 512)

    w_main, w_gate, hp, dsk, gbias = _prep_weights(w_in, ssd_dt_bias, ssd_a_log, ssd_d,
                                                   mlstm_i_bias, mlstm_f_bias)
    w_out_b = w_out.astype(BF16)
    w_up_b = w_up.astype(BF16)
    w_down_b = w_down.astype(BF16)

    row3 = lambda p: p.reshape(depth, 1, -1)
    ssd_cb, ssd_nw, ml_nw, n2w, ffn_cb = (row3(p) for p in (ssd_conv_b, ssd_norm_w, mlstm_norm_w,
                                                            norm2_w, ffn_conv_b))

    shifts = _conv_shift_matrices()

    xf = x.reshape(m, d)
    h = _rmsnorm(xf, norm1_w[0], tm)
    for l in range(depth):
        proj, gates = _inproj(h, w_main, w_gate, l, tm, N_MAIN // 4)
        proj3 = proj.reshape(bsz, seq, N_MAIN)
        gates3 = gates.reshape(bsz, seq, N_GATE)
        y_ssd = _ssd_mixer(proj3, gates3, ssd_conv_w, ssd_cb, hp, dsk, ssd_nw, shifts, l)
        y_ml = _mlstm_mixer(proj3, gates3, gbias, ml_nw, l)
        x1, h2 = _outproj(xf, y_ssd.reshape(m, SSD_WIDTH), y_ml.reshape(m, ML_WIDTH), w_out_b, n2w, l,
                          min(m, 512))
        last = l == depth - 1
        nw = norm_f_w if last else norm1_w[l + 1]
        xf, h = _ffn(x1, h2, w_up_b, ffn_conv_w, ffn_cb, w_down_b, nw.reshape(1, d), l, seq, t_ffn,
                     512, F32 if last else BF16)
    return h.reshape(bsz, seq, d)
```

```python
import functools

import jax
import jax.numpy as jnp
import numpy as np
from jax import lax
from jax.experimental import pallas as pl
from jax.experimental.pallas import tpu as pltpu

EPS = 1e-6
NEG = -1e30
LOG2E = 1.4426950408889634
F32 = jnp.float32
BF16 = jnp.bfloat16

D_MODEL = 2048
SSD_WIDTH = 2048
SSD_HEAD_DIM = 64
SSD_HEADS = 32
SSD_GROUPS = 4
SSD_STATE = 128
SSD_HEADS_PER_GROUP = SSD_HEADS // SSD_GROUPS
SSD_GROUP_WIDTH = SSD_WIDTH // SSD_GROUPS
SSD_CONV = 5
ML_HEADS = 8
ML_DK = 128
ML_DV = 256
ML_WIDTH = ML_HEADS * ML_DV
ML_QK_SCALE = ML_DK ** -0.5
ML_HEADS_PER_STEP = 2
D_FF = 5632
FFN_CONV = 3
CHUNK = 128
LANES = 128
BF16_ROWS = 16
CONV_ROWS = 64
CONV_PAD = 32
CONV_WIN = CONV_ROWS + 2 * CONV_PAD
N_MAIN = 2 * SSD_WIDTH + 2 * SSD_GROUPS * SSD_STATE + 2 * ML_HEADS * ML_DK + 2 * ML_WIDTH
N_GATE = LANES
ML_GATE_LANE0 = 2 * SSD_HEADS
VMEM_LIMIT = 56 * 1024 * 1024


def _params(*sem):
    return pltpu.CompilerParams(dimension_semantics=sem, vmem_limit_bytes=VMEM_LIMIT)


def _sigmoid(x):
    return 0.5 + 0.5 * jnp.tanh(0.5 * x)


def _silu(x):
    hx = 0.5 * x
    return hx + hx * jnp.tanh(hx)


def _softplus(x):
    return jnp.maximum(x, 0.0) + jnp.log(1.0 + jnp.exp(-jnp.abs(x)))


def _dot(a, b):
    return jnp.dot(a, b, preferred_element_type=F32)


def _split3(x):
    hi = x.astype(BF16)
    r = x - hi.astype(F32)
    mid = r.astype(BF16)
    lo = (r - mid.astype(F32)).astype(BF16)
    return hi, mid, lo


def _cumdot(tri3, x):
    return _dot(tri3, jnp.concatenate(_split3(x), axis=0))


def _tri3(fwd):
    row = lax.broadcasted_iota(jnp.int32, (CHUNK, 3 * CHUNK), 0)
    col = lax.broadcasted_iota(jnp.int32, (CHUNK, 3 * CHUNK), 1) % CHUNK
    keep = (col <= row) if fwd else (col >= row)
    return jnp.where(keep, 1.0, 0.0).astype(BF16)


def _lane_col(x, lane_iota, lane):
    return jnp.sum(jnp.where(lane_iota == lane, x, 0.0), axis=1, keepdims=True)


def _rmsnorm_kernel(x_ref, w_ref, o_ref):
    x = x_ref[...]
    ms = jnp.mean(x * x, axis=1, keepdims=True)
    o_ref[...] = (x * lax.rsqrt(ms + EPS) * w_ref[...]).astype(o_ref.dtype)


def _rmsnorm(x, w, tm):
    m, d = x.shape
    return pl.pallas_call(
        _rmsnorm_kernel,
        grid=(m // tm,),
        in_specs=[pl.BlockSpec((tm, d), lambda i: (i, 0)), pl.BlockSpec((1, d), lambda i: (0, 0))],
        out_specs=pl.BlockSpec((tm, d), lambda i: (i, 0)),
        out_shape=jax.ShapeDtypeStruct((m, d), BF16),
        compiler_params=_params("parallel"),
        name="rmsnorm_in",
    )(x, w.reshape(1, d))


def _inproj_kernel(h_ref, w_ref, wg_ref, p_ref, g_ref):
    h = h_ref[...]
    p_ref[...] = _dot(h, w_ref[...]).astype(p_ref.dtype)

    @pl.when(pl.program_id(1) == 0)
    def _():
        g_ref[...] = _dot(h, wg_ref[...])


def _inproj(h, w_main, w_gate, layer, tm, tn):
    m, d = h.shape
    return pl.pallas_call(
        _inproj_kernel,
        grid=(m // tm, N_MAIN // tn),
        in_specs=[pl.BlockSpec((tm, d), lambda i, j: (i, 0)),
                  pl.BlockSpec((None, d, tn), lambda i, j: (layer, 0, j)),
                  pl.BlockSpec((None, d, N_GATE), lambda i, j: (layer, 0, 0))],
        out_specs=[pl.BlockSpec((tm, tn), lambda i, j: (i, j)),
                   pl.BlockSpec((tm, N_GATE), lambda i, j: (i, 0))],
        out_shape=[jax.ShapeDtypeStruct((m, N_MAIN), BF16),
                   jax.ShapeDtypeStruct((m, N_GATE), F32)],
        compiler_params=_params("parallel", "arbitrary"),
        name="in_proj",
    )(h, w_main, w_gate)


def _ssd_chunk(c, fwd, refs, consts):
    (xb_ref, bt_ref, cc_ref, acs_ref, rt_ref, sel_ref, y_ref, s_ref) = refs
    (mask, lane0, left) = consts
    rows = pl.ds(pl.multiple_of(c * CHUNK, CHUNK), CHUNK)
    acs = acs_ref[rows, :]
    cc = cc_ref[rows, :]
    b_t = bt_ref[c]
    scores = _dot(cc, b_t.astype(BF16))
    s_prev = s_ref[...]
    y_off = _dot(cc, s_prev.astype(BF16))
    base = 0 if fwd else SSD_HEADS_PER_GROUP
    edge = CHUNK - 1 if fwd else 0
    for q in range(SSD_HEADS_PER_GROUP // 2):
        cols = slice(q * LANES, (q + 1) * LANES)
        xb = xb_ref[rows, cols]
        zero = jnp.zeros_like(xb)
        xbd = jnp.concatenate([jnp.where(left, xb, zero), jnp.where(left, zero, xb)], axis=0)
        col_a, last, ms, bws = [], [], [], []
        for e in range(2):
            hd = base + 2 * q + e
            ca = jnp.sum(acs * sel_ref[hd], axis=1, keepdims=True)
            r_row = rt_ref[c, pl.ds(lane0 + hd, 1), :]
            la = ca[edge:edge + 1, :]
            ms.append((scores * jnp.exp2(jnp.where(mask, ca + r_row, NEG))).astype(BF16))
            bws.append((b_t * jnp.exp2(r_row + la)).astype(BF16))
            col_a.append(ca)
            last.append(la)
        lhs = jnp.concatenate([jnp.concatenate(ms, axis=1), jnp.concatenate(bws, axis=1)], axis=0)
        res = _dot(lhs, xbd)
        a_sel = jnp.where(left, col_a[0], col_a[1])
        y_ref[rows, cols] = res[:CHUNK] + jnp.exp2(a_sel) * y_off[:, cols]
        last_sel = jnp.where(left[0:1, :], last[0], last[1])
        s_ref[:, cols] = jnp.exp2(last_sel) * s_prev[:, cols] + res[CHUNK:]


def _ssd_kernel(xs_ref, z_ref, b_ref, c_ref, gt_ref, cwx_ref, cwb_ref, cwc_ref, cbx_ref, cbb_ref,
                cbc_ref, hp_ref, dsk_ref, nw_ref, sh_ref, o_ref,
                stage_ref, xc_ref, xb_ref, bt_ref, cc_ref, acs_ref, rt_ref, yf_ref, yb_ref,
                sf_ref, sb_ref, tri_ref, sel_ref):
    seq = xs_ref.shape[0]
    nc = seq // CHUNK
    rb = min(seq, 256)
    gw = SSD_GROUP_WIDTH
    n = SSD_STATE
    width = gw + 2 * n

    stage_ref[0:CONV_PAD, :] = jnp.zeros((CONV_PAD, width), BF16)
    stage_ref[CONV_PAD + seq:2 * CONV_PAD + seq, :] = jnp.zeros((CONV_PAD, width), BF16)
    for r in range(0, seq, rb):
        dst = slice(CONV_PAD + r, CONV_PAD + r + rb)
        stage_ref[dst, 0:gw] = xs_ref[r:r + rb, :]
        stage_ref[dst, gw:gw + n] = b_ref[r:r + rb, :]
        stage_ref[dst, gw + n:width] = c_ref[r:r + rb, :]

    taps = [k for k in range(SSD_CONV) if k != SSD_CONV // 2]

    cb = CONV_ROWS

    def conv(r, c0, c1, weights, bias):
        win = stage_ref[r:r + CONV_WIN, c0:c1]
        shifted = _dot(sh_ref[...], win)
        centre = stage_ref[CONV_PAD + r:CONV_PAD + r + cb, c0:c1].astype(F32)
        acc = bias + weights[SSD_CONV // 2] * centre
        for i, k in enumerate(taps):
            acc = acc + weights[k] * shifted[i * cb:(i + 1) * cb]
        return _silu(acc)

    half = 2 * LANES
    wbc = [jnp.concatenate([cwb_ref[k:k + 1, :], cwc_ref[k:k + 1, :]], axis=1) for k in range(SSD_CONV)]
    bbc = jnp.concatenate([cbb_ref[...], cbc_ref[...]], axis=1)
    for c in range(nc):
        bcs = []
        for r in range(c * CHUNK, (c + 1) * CHUNK, cb):
            for c0 in range(0, gw, half):
                wx = [cwx_ref[k:k + 1, c0:c0 + half] for k in range(SSD_CONV)]
                xc = conv(r, c0, c0 + half, wx, cbx_ref[:, c0:c0 + half])
                xc_ref[r:r + cb, c0:c0 + half] = xc
                xb_ref[r:r + cb, c0:c0 + half] = xc.astype(BF16)
            bc = conv(r, gw, width, wbc, bbc)
            cc_ref[r:r + cb, :] = bc[:, n:].astype(BF16)
            bcs.append(bc[:, :n])
        bt_ref[c] = jnp.concatenate(bcs, axis=0).T

    row = lax.broadcasted_iota(jnp.int32, (CHUNK, CHUNK), 0)
    col = lax.broadcasted_iota(jnp.int32, (CHUNK, CHUNK), 1)
    left = col < SSD_HEAD_DIM

    hg = SSD_HEADS_PER_GROUP
    lane0 = pl.program_id(1) * (2 * hg)
    fwd_lane = (col % (2 * hg)) < hg
    for hd in range(2 * hg):
        sel_ref[hd] = jnp.where(col == lane0 + hd, 1.0, 0.0)
    tri_ref[0] = _tri3(True)
    tri_ref[1] = _tri3(False)

    def cumsum_body(c, carry):
        rows = pl.ds(pl.multiple_of(c * CHUNK, CHUNK), CHUNK)
        dt = _softplus(gt_ref[rows, :] + hp_ref[0:1, :])
        a_dt = dt * (-jnp.exp(hp_ref[1:2, :]))
        acs = jnp.where(fwd_lane, _cumdot(tri_ref[0], a_dt), _cumdot(tri_ref[1], a_dt))
        acs_ref[rows, :] = acs * LOG2E
        rt_ref[c] = ((jnp.log(dt) - acs) * LOG2E).T
        return carry

    lax.fori_loop(0, nc, cumsum_body, 0, unroll=4)

    sf_ref[...] = jnp.zeros_like(sf_ref)
    sb_ref[...] = jnp.zeros_like(sb_ref)

    consts_f = (col <= row, lane0, left)
    consts_b = (col >= row, lane0, left)
    refs_f = (xb_ref, bt_ref, cc_ref, acs_ref, rt_ref, sel_ref, yf_ref, sf_ref)
    refs_b = (xb_ref, bt_ref, cc_ref, acs_ref, rt_ref, sel_ref, yb_ref, sb_ref)

    def body(j, carry):
        _ssd_chunk(j, True, refs_f, consts_f)
        _ssd_chunk(nc - 1 - j, False, refs_b, consts_b)
        return carry

    lax.fori_loop(0, nc, body, 0, unroll=2)

    for r in range(0, seq, rb):
        rows = slice(r, r + rb)
        y = yf_ref[rows, :] + yb_ref[rows, :] + xc_ref[rows, :] * dsk_ref[...]
        y = y * _silu(z_ref[rows, :].astype(F32))
        ms = jnp.mean(y * y, axis=1, keepdims=True)
        o_ref[rows, :] = (y * lax.rsqrt(ms + EPS) * nw_ref[...]).astype(o_ref.dtype)


def _ssd_mixer(proj, gates, cw, cb, hp, dsk, nw, shifts, layer):
    bsz, seq, _ = proj.shape
    gw, n, g = SSD_GROUP_WIDTH, SSD_STATE, SSD_GROUPS
    xs0, z0 = 0, SSD_WIDTH // gw
    b0 = 2 * SSD_WIDTH // n
    c0 = b0 + g
    cwb0 = SSD_WIDTH // n
    in_specs = [
        pl.BlockSpec((None, seq, gw), lambda b, i: (b, 0, xs0 + i)),
        pl.BlockSpec((None, seq, gw), lambda b, i: (b, 0, z0 + i)),
        pl.BlockSpec((None, seq, n), lambda b, i: (b, 0, b0 + i)),
        pl.BlockSpec((None, seq, n), lambda b, i: (b, 0, c0 + i)),
        pl.BlockSpec((None, seq, LANES), lambda b, i: (b, 0, 0)),
        pl.BlockSpec((None, SSD_CONV, gw), lambda b, i: (layer, 0, i)),
        pl.BlockSpec((None, SSD_CONV, n), lambda b, i: (layer, 0, cwb0 + i)),
        pl.BlockSpec((None, SSD_CONV, n), lambda b, i: (layer, 0, cwb0 + g + i)),
        pl.BlockSpec((None, 1, gw), lambda b, i: (layer, 0, i)),
        pl.BlockSpec((None, 1, n), lambda b, i: (layer, 0, cwb0 + i)),
        pl.BlockSpec((None, 1, n), lambda b, i: (layer, 0, cwb0 + g + i)),
        pl.BlockSpec((None, 8, LANES), lambda b, i: (layer, 0, 0)),
        pl.BlockSpec((None, 1, gw), lambda b, i: (layer, 0, i)),
        pl.BlockSpec((None, 1, gw), lambda b, i: (layer, 0, i)),
        pl.BlockSpec(shifts.shape, lambda b, i: (0, 0)),
    ]
    scratch = [
        pltpu.VMEM((seq + 2 * CONV_PAD, gw + 2 * n), BF16),
        pltpu.VMEM((seq, gw), F32),
        pltpu.VMEM((seq, gw), BF16),
        pltpu.VMEM((seq // CHUNK, n, CHUNK), F32),
        pltpu.VMEM((seq, n), BF16),
        pltpu.VMEM((seq, LANES), F32),
        pltpu.VMEM((seq // CHUNK, LANES, CHUNK), F32),
        pltpu.VMEM((seq, gw), F32),
        pltpu.VMEM((seq, gw), F32),
        pltpu.VMEM((n, gw), F32),
        pltpu.VMEM((n, gw), F32),
        pltpu.VMEM((2, CHUNK, 3 * CHUNK), BF16),
        pltpu.VMEM((2 * SSD_HEADS_PER_GROUP, CHUNK, LANES), F32),
    ]
    return pl.pallas_call(
        _ssd_kernel,
        grid=(bsz, g),
        in_specs=in_specs,
        out_specs=pl.BlockSpec((None, seq, gw), lambda b, i: (b, 0, i)),
        out_shape=jax.ShapeDtypeStruct((bsz, seq, SSD_WIDTH), BF16),
        scratch_shapes=scratch,
        compiler_params=_params("parallel", "parallel"),
        name="ssd_mixer",
    )(proj, proj, proj, proj, gates, cw, cw, cw, cb, cb, cb, hp, dsk, nw, shifts)


def _mlstm_gates(c, fwd, head, hh, m_state, refs, consts):
    (q_ref, k_t_ref, gi_t_ref, bs_ref, bs_t_ref, sb_ref, wi_ref, en_ref, kw_ref, dec_ref) = refs
    (mask, lane_iota) = consts
    nc = dec_ref.shape[1] // 2
    rows = pl.ds(pl.multiple_of(c * CHUNK, CHUNK), CHUNK)
    d = 0 if fwd else 1
    edge = CHUNK - 1 if fwd else 0
    f_lane = ML_GATE_LANE0 + (2 + d) * ML_HEADS + head
    li_row = gi_t_ref[c, pl.ds(ML_GATE_LANE0 + d * ML_HEADS + head, 1), :]
    b_row = bs_t_ref[d, c, pl.ds(f_lane, 1), :]
    d_row = li_row - b_row
    b_col = _lane_col(bs_ref[d, rows, :], lane_iota, f_lane)
    dm = jnp.where(mask, d_row, NEG)
    g = jnp.maximum(jnp.max(dm, axis=1, keepdims=True), m_state)
    k_t = k_t_ref[hh, c]
    q = q_ref[rows, hh * ML_DK:(hh + 1) * ML_DK]
    s = _dot(q, k_t) * (jnp.exp(dm - g) * ML_QK_SCALE)
    sb_ref[hh, d, rows, :] = s.astype(BF16)
    wi_ref[hh, d, rows, :] = jnp.broadcast_to(jnp.exp(m_state - g) * ML_QK_SCALE, (CHUNK, LANES))
    en_ref[hh, d, rows, :] = jnp.broadcast_to(jnp.exp(-(b_col + g)), (CHUNK, LANES))
    g_last = g[edge:edge + 1, :]
    kw_ref[hh, d, c] = (k_t.astype(F32) * jnp.exp(d_row - g_last)).astype(BF16)
    dec_ref[hh, pl.ds(d * nc + c, 1), :] = jnp.broadcast_to(jnp.exp(m_state - g_last),
                                                            (1, dec_ref.shape[2]))
    return b_col[edge:edge + 1, :] + g_last


def _mlstm_chunk(c, fwd, hh, ones, refs):
    (q_ref, v_ref, sb_ref, wi_ref, en_ref, kw_ref, dec_ref, h_ref, cs_ref) = refs
    nc = dec_ref.shape[1] // 2
    rows = pl.ds(pl.multiple_of(c * CHUNK, CHUNK), CHUNK)
    d = 0 if fwd else 1
    vcols = slice(hh * ML_DV, (hh + 1) * ML_DV)
    v_aug = jnp.concatenate([v_ref[rows, vcols], ones], axis=1)
    c_aug = cs_ref[hh]
    wi = wi_ref[hh, d, rows, :]
    both = _dot(jnp.concatenate([sb_ref[hh, d, rows, :], kw_ref[hh, d, c]], axis=0), v_aug)
    q = q_ref[rows, hh * ML_DK:(hh + 1) * ML_DK]
    na = both[:CHUNK] + jnp.concatenate([wi, wi, wi], axis=1) * _dot(q, c_aug.astype(BF16))
    r = 1.0 / jnp.maximum(jnp.abs(na[:, ML_DV:]), en_ref[hh, d, rows, :])
    h_ref[rows, vcols] = na[:, :ML_DV] * jnp.concatenate([r, r], axis=1)
    cs_ref[hh] = dec_ref[hh, pl.ds(d * nc + c, 1), :] * c_aug + both[CHUNK:]


def _mlstm_kernel(q_ref, k_ref, v_ref, o_ref, gt_ref, gb_ref, nw_ref, y_ref,
                  gi_t_ref, bs_ref, bs_t_ref, k_t_ref, sb_ref, wi_ref, en_ref, kw_ref, dec_ref,
                  hf_ref, hb_ref, cf_ref, cb_ref, tri_ref):
    seq = q_ref.shape[0]
    nc = seq // CHUNK
    rb = min(seq, 256)
    hps = ML_HEADS_PER_STEP
    head0 = pl.program_id(1) * hps

    @pl.when(head0 == 0)
    def _():
        tri_ref[0] = _tri3(True)
        tri_ref[1] = _tri3(False)

        def cumsum_body(c, carry):
            rows = pl.ds(pl.multiple_of(c * CHUNK, CHUNK), CHUNK)
            g = gt_ref[rows, :] + gb_ref[...]
            gi_t_ref[c] = g.T
            lf = -_softplus(-g)
            b_f = _cumdot(tri_ref[0], lf)
            b_b = _cumdot(tri_ref[1], lf)
            bs_ref[0, rows, :] = b_f
            bs_ref[1, rows, :] = b_b
            bs_t_ref[0, c] = b_f.T
            bs_t_ref[1, c] = b_b.T
            return carry

        lax.fori_loop(0, nc, cumsum_body, 0, unroll=2)

    def transpose_body(c, carry):
        rows = pl.ds(pl.multiple_of(c * CHUNK, CHUNK), CHUNK)
        for hh in range(hps):
            k_t_ref[hh, c] = k_ref[rows, hh * ML_DK:(hh + 1) * ML_DK].astype(F32).T.astype(BF16)
        return carry

    lax.fori_loop(0, nc, transpose_body, 0, unroll=2)
    cf_ref[...] = jnp.zeros_like(cf_ref)
    cb_ref[...] = jnp.zeros_like(cb_ref)

    row = lax.broadcasted_iota(jnp.int32, (CHUNK, CHUNK), 0)
    col = lax.broadcasted_iota(jnp.int32, (CHUNK, CHUNK), 1)
    gate_refs = (q_ref, k_t_ref, gi_t_ref, bs_ref, bs_t_ref, sb_ref, wi_ref, en_ref, kw_ref, dec_ref)

    def gate_body(j, carry):
        out = []
        for hh in range(hps):
            out.append(_mlstm_gates(j, True, head0 + hh, hh, carry[2 * hh], gate_refs,
                                    (col <= row, col)))
            out.append(_mlstm_gates(nc - 1 - j, False, head0 + hh, hh, carry[2 * hh + 1], gate_refs,
                                    (col >= row, col)))
        return tuple(out)

    m_init = jnp.full((1, 1), NEG, F32)
    lax.fori_loop(0, nc, gate_body, (m_init,) * (2 * hps), unroll=2)

    ones = jnp.ones((CHUNK, LANES), BF16)
    shared = (q_ref, v_ref, sb_ref, wi_ref, en_ref, kw_ref, dec_ref)

    def body(j, carry):
        for hh in range(hps):
            _mlstm_chunk(j, True, hh, ones, shared + (hf_ref, cf_ref))
            _mlstm_chunk(nc - 1 - j, False, hh, ones, shared + (hb_ref, cb_ref))
        return carry

    lax.fori_loop(0, nc, body, 0)

    for r in range(0, seq, rb):
        rows = slice(r, r + rb)
        for hh in range(hps):
            vcols = slice(hh * ML_DV, (hh + 1) * ML_DV)
            hs = hf_ref[rows, vcols] + hb_ref[rows, vcols]
            ms = jnp.mean(hs * hs, axis=1, keepdims=True)
            hn = hs * lax.rsqrt(ms + EPS) * nw_ref[:, vcols]
            gate = _sigmoid(o_ref[rows, vcols].astype(F32))
            y_ref[rows, vcols] = (gate * hn).astype(y_ref.dtype)


def _mlstm_mixer(proj, gates, gbias, nw, layer):
    bsz, seq, _ = proj.shape
    hps = ML_HEADS_PER_STEP
    qk_w, v_w = hps * ML_DK, hps * ML_DV
    q_off = 2 * SSD_WIDTH + 2 * SSD_GROUPS * SSD_STATE
    q0 = q_off // qk_w
    k0 = q0 + ML_HEADS // hps
    v0 = (q_off + 2 * ML_HEADS * ML_DK) // v_w
    o0 = v0 + ML_HEADS // hps
    in_specs = [
        pl.BlockSpec((None, seq, qk_w), lambda b, h: (b, 0, q0 + h)),
        pl.BlockSpec((None, seq, qk_w), lambda b, h: (b, 0, k0 + h)),
        pl.BlockSpec((None, seq, v_w), lambda b, h: (b, 0, v0 + h)),
        pl.BlockSpec((None, seq, v_w), lambda b, h: (b, 0, o0 + h)),
        pl.BlockSpec((None, seq, LANES), lambda b, h: (b, 0, 0)),
        pl.BlockSpec((None, 1, LANES), lambda b, h: (layer, 0, 0)),
        pl.BlockSpec((None, 1, v_w), lambda b, h: (layer, 0, h)),
    ]
    nc = seq // CHUNK
    aug = ML_DV + LANES
    scratch = [
        pltpu.VMEM((nc, LANES, CHUNK), F32),
        pltpu.VMEM((2, seq, LANES), F32),
        pltpu.VMEM((2, nc, LANES, CHUNK), F32),
        pltpu.VMEM((hps, nc, ML_DK, CHUNK), BF16),
        pltpu.VMEM((hps, 2, seq, CHUNK), BF16),
        pltpu.VMEM((hps, 2, seq, LANES), F32),
        pltpu.VMEM((hps, 2, seq, LANES), F32),
        pltpu.VMEM((hps, 2, nc, ML_DK, CHUNK), BF16),
        pltpu.VMEM((hps, 2 * nc, aug), F32),
        pltpu.VMEM((seq, v_w), F32),
        pltpu.VMEM((seq, v_w), F32),
        pltpu.VMEM((hps, ML_DK, aug), F32),
        pltpu.VMEM((hps, ML_DK, aug), F32),
        pltpu.VMEM((2, CHUNK, 3 * CHUNK), BF16),
    ]
    return pl.pallas_call(
        _mlstm_kernel,
        grid=(bsz, ML_HEADS // hps),
        in_specs=in_specs,
        out_specs=pl.BlockSpec((None, seq, v_w), lambda b, h: (b, 0, h)),
        out_shape=jax.ShapeDtypeStruct((bsz, seq, ML_WIDTH), BF16),
        scratch_shapes=scratch,
        compiler_params=_params("parallel", "arbitrary"),
        name="mlstm_mixer",
    )(proj, proj, proj, proj, gates, gbias, nw)


def _outproj_kernel(x_ref, ys_ref, ym_ref, ws_ref, wm_ref, nw_ref, x1_ref, h_ref):
    x1 = x_ref[...] + _dot(ys_ref[...], ws_ref[...]) + _dot(ym_ref[...], wm_ref[...])
    x1_ref[...] = x1
    ms = jnp.mean(x1 * x1, axis=1, keepdims=True)
    h_ref[...] = (x1 * lax.rsqrt(ms + EPS) * nw_ref[...]).astype(h_ref.dtype)


def _outproj(x, y_ssd, y_ml, w_out, nw, layer, tm):
    m, d = x.shape
    return pl.pallas_call(
        _outproj_kernel,
        grid=(m // tm,),
        in_specs=[pl.BlockSpec((tm, d), lambda i: (i, 0)),
                  pl.BlockSpec((tm, SSD_WIDTH), lambda i: (i, 0)),
                  pl.BlockSpec((tm, ML_WIDTH), lambda i: (i, 0)),
                  pl.BlockSpec((None, SSD_WIDTH, d), lambda i: (layer, 0, 0),
                               pipeline_mode=pl.Buffered(1)),
                  pl.BlockSpec((None, ML_WIDTH, d), lambda i: (layer, 1, 0),
                               pipeline_mode=pl.Buffered(1)),
                  pl.BlockSpec((None, 1, d), lambda i: (layer, 0, 0))],
        out_specs=[pl.BlockSpec((tm, d), lambda i: (i, 0)),
                   pl.BlockSpec((tm, d), lambda i: (i, 0))],
        out_shape=[jax.ShapeDtypeStruct((m, d), F32), jax.ShapeDtypeStruct((m, d), BF16)],
        compiler_params=_params("parallel"),
        name="out_proj",
    )(x, y_ssd, y_ml, w_out, w_out, nw)


def _ffn_kernel(tiles_per_seq, h_ref, hp_ref, hn_ref, x_ref, wg_ref, wv_ref, cwg_ref, cwv_ref,
                cbg_ref, cbv_ref, wd_ref, nw_ref, x2_ref, o_ref, hext_ref, ug_ref, uv_ref, acc_ref):
    i = pl.program_id(0)
    j = pl.program_id(1)
    t = h_ref.shape[0]
    halo = BF16_ROWS

    @pl.when(j == 0)
    def _():
        pos = i % tiles_per_seq
        prev = hp_ref[...]
        nxt = hn_ref[...]
        hext_ref[0:halo, :] = jnp.where(pos == 0, jnp.zeros_like(prev), prev)
        hext_ref[halo:halo + t, :] = h_ref[...]
        hext_ref[halo + t:2 * halo + t, :] = jnp.where(pos == tiles_per_seq - 1, jnp.zeros_like(nxt), nxt)
        acc_ref[...] = jnp.zeros_like(acc_ref)

    hext = hext_ref[...]
    ug_ref[...] = _dot(hext, wg_ref[...])
    uv_ref[...] = _dot(hext, wv_ref[...])

    def conv(u_ref, w_ref, b_ref):
        acc = b_ref[...] + w_ref[0:1, :] * u_ref[halo - 1:halo - 1 + t, :]
        acc = acc + w_ref[1:2, :] * u_ref[halo:halo + t, :]
        return acc + w_ref[2:3, :] * u_ref[halo + 1:halo + 1 + t, :]

    act = _silu(conv(ug_ref, cwg_ref, cbg_ref)) * conv(uv_ref, cwv_ref, cbv_ref)
    acc_ref[...] += _dot(act.astype(BF16), wd_ref[...])

    @pl.when(j == pl.num_programs(1) - 1)
    def _():
        x2 = x_ref[...] + acc_ref[...]
        x2_ref[...] = x2
        ms = jnp.mean(x2 * x2, axis=1, keepdims=True)
        o_ref[...] = (x2 * lax.rsqrt(ms + EPS) * nw_ref[...]).astype(o_ref.dtype)


def _ffn(x1, h2, w_up, cw, cb, w_down, nw, layer, seq, t, f, out_dtype):
    m, d = x1.shape
    nj = D_FF // f
    tiles_per_seq = seq // t
    hb = t // BF16_ROWS
    last_blk = m // BF16_ROWS - 1
    in_specs = [
        pl.BlockSpec((t, d), lambda i, j: (i, 0)),
        pl.BlockSpec((BF16_ROWS, d), lambda i, j: (jnp.maximum(i * hb - 1, 0), 0)),
        pl.BlockSpec((BF16_ROWS, d), lambda i, j: (jnp.minimum((i + 1) * hb, last_blk), 0)),
        pl.BlockSpec((t, d), lambda i, j: (i, 0)),
        pl.BlockSpec((None, d, f), lambda i, j: (layer, 0, j)),
        pl.BlockSpec((None, d, f), lambda i, j: (layer, 0, nj + j)),
        pl.BlockSpec((None, FFN_CONV, f), lambda i, j: (layer, 0, j)),
        pl.BlockSpec((None, FFN_CONV, f), lambda i, j: (layer, 0, nj + j)),
        pl.BlockSpec((None, 1, f), lambda i, j: (layer, 0, j)),
        pl.BlockSpec((None, 1, f), lambda i, j: (layer, 0, nj + j)),
        pl.BlockSpec((None, f, d), lambda i, j: (layer, j, 0)),
        pl.BlockSpec((1, d), lambda i, j: (0, 0)),
    ]
    scratch = [
        pltpu.VMEM((t + 2 * BF16_ROWS, d), BF16),
        pltpu.VMEM((t + 2 * BF16_ROWS, f), F32),
        pltpu.VMEM((t + 2 * BF16_ROWS, f), F32),
        pltpu.VMEM((t, d), F32),
    ]
    return pl.pallas_call(
        functools.partial(_ffn_kernel, tiles_per_seq),
        grid=(m // t, nj),
        in_specs=in_specs,
        out_specs=[pl.BlockSpec((t, d), lambda i, j: (i, 0)),
                   pl.BlockSpec((t, d), lambda i, j: (i, 0))],
        out_shape=[jax.ShapeDtypeStruct((m, d), F32), jax.ShapeDtypeStruct((m, d), out_dtype)],
        scratch_shapes=scratch,
        compiler_params=_params("parallel", "arbitrary"),
        name="conv_ffn",
    )(h2, h2, h2, x1, w_up, w_up, cw, cw, cb, cb, w_down, nw)


def _pad_lanes(a, width):
    return jnp.pad(a, [(0, 0)] * (a.ndim - 1) + [(0, width - a.shape[-1])])


def _prep_weights(w_in, ssd_dt_bias, ssd_a_log, ssd_d, mlstm_i_bias, mlstm_f_bias):
    depth = w_in.shape[0]
    sizes = (SSD_WIDTH, SSD_WIDTH, SSD_GROUPS * SSD_STATE, SSD_GROUPS * SSD_STATE, 2 * SSD_HEADS,
             ML_HEADS * ML_DK, ML_HEADS * ML_DK, ML_WIDTH, ML_WIDTH, 2 * ML_HEADS, 2 * ML_HEADS)
    offs = np.cumsum((0,) + sizes)
    part = lambda n: w_in[:, :, offs[n]:offs[n + 1]]
    dt, ig, fg = part(4), part(9), part(10)
    n_a = 2 * SSD_WIDTH + 2 * SSD_GROUPS * SSD_STATE
    w_main = jnp.concatenate([w_in[:, :, :n_a].astype(BF16),
                              w_in[:, :, offs[5]:offs[9]].astype(BF16)],
                             axis=-1)
    hg = SSD_HEADS_PER_GROUP
    by_group = lambda p, lead: jnp.swapaxes(p.reshape(lead + (2, SSD_GROUPS, hg)), -3, -2).reshape(
        lead + (2 * SSD_HEADS,))
    w_gate = _pad_lanes(jnp.concatenate([by_group(dt, (depth, D_MODEL)), ig, fg], axis=-1),
                        LANES).astype(BF16)
    hp = jnp.stack([_pad_lanes(by_group(ssd_dt_bias, (depth,)), LANES),
                    _pad_lanes(by_group(ssd_a_log, (depth,)), LANES)], axis=1)
    hp = jnp.pad(hp, ((0, 0), (0, 6), (0, 0)))
    dsk = jnp.repeat(ssd_d, SSD_HEAD_DIM, axis=-1).reshape(depth, 1, SSD_WIDTH)
    gbias = _pad_lanes(jnp.concatenate([jnp.zeros((depth, ML_GATE_LANE0), F32),
                                        mlstm_i_bias.reshape(depth, 2 * ML_HEADS),
                                        mlstm_f_bias.reshape(depth, 2 * ML_HEADS)], axis=-1),
                       LANES).reshape(depth, 1, LANES)
    return w_main, w_gate, hp, dsk, gbias


def _conv_shift_matrices():
    taps = [k for k in range(SSD_CONV) if k != SSD_CONV // 2]
    sh = np.zeros((len(taps) * CONV_ROWS, CONV_WIN), np.float32)
    for i, k in enumerate(taps):
        sh[i * CONV_ROWS + np.arange(CONV_ROWS), np.arange(CONV_ROWS) + k - SSD_CONV // 2 + CONV_PAD] = 1.0
    return jnp.asarray(sh, BF16)


def kernel(x, norm1_w, w_in, ssd_conv_w, ssd_conv_b, ssd_dt_bias, ssd_a_log, ssd_d, ssd_norm_w,
           mlstm_i_bias, mlstm_f_bias, mlstm_norm_w, w_out, norm2_w, w_up, ffn_conv_w, ffn_conv_b,
           w_down, norm_f_w):
    bsz, seq, d = x.shape
    depth = w_in.shape[0]
    m = bsz * seq
    assert d == D_MODEL and seq % CHUNK == 0
    tm = min(m, 1024)
    t_ffn = min(seq, 512)

    w_main, w_gate, hp, dsk, gbias = _prep_weights(w_in, ssd_dt_bias, ssd_a_log, ssd_d,
                                                   mlstm_i_bias, mlstm_f_bias)
    w_out_b = w_out.astype(BF16)
    w_up_b = w_up.astype(BF16)
    w_down_b = w_down.astype(BF16)

    row3 = lambda p: p.reshape(depth, 1, -1)
    ssd_cb, ssd_nw, ml_nw, n2w, ffn_cb = (row3(p) for p in (ssd_conv_b, ssd_norm_w, mlstm_norm_w,
                                                            norm2_w, ffn_conv_b))

    shifts = _conv_shift_matrices()

    xf = x.reshape(m, d)
    h = _rmsnorm(xf, norm1_w[0], tm)
    for l in range(depth):
        proj, gates = _inproj(h, w_main, w_gate, l, tm, N_MAIN // 4)
        proj3 = proj.reshape(bsz, seq, N_MAIN)
        gates3 = gates.reshape(bsz, seq, N_GATE)
        y_ssd = _ssd_mixer(proj3, gates3, ssd_conv_w, ssd_cb, hp, dsk, ssd_nw, shifts, l)
        y_ml = _mlstm_mixer(proj3, gates3, gbias, ml_nw, l)
        x1, h2 = _outproj(xf, y_ssd.reshape(m, SSD_WIDTH), y_ml.reshape(m, ML_WIDTH), w_out_b, n2w, l,
                          min(m, 512))
        last = l == depth - 1
        nw = norm_f_w if last else norm1_w[l + 1]
        xf, h = _ffn(x1, h2, w_up_b, ffn_conv_w, ffn_cb, w_down_b, nw.reshape(1, d), l, seq, t_ffn,
                     512, F32 if last else BF16)
    return h.reshape(bsz, seq, d)
```
